```python
import math
import jax, jax.numpy as jnp
from jax import lax
import numpy as np

D_MODEL = 1024
BATCH = 4
SEQ = 8192
DEPTH = 4

SSM_WIDTH = D_MODEL // 2
SSM_GROUP = 16
SSM_GROUPS = SSM_WIDTH // SSM_GROUP
SSM_STATE = 64
MLA_HEADS = 8
QK_NOPE = 64
QK_ROPE = 32
QK_HEAD = QK_NOPE + QK_ROPE
V_HEAD = 64
Q_LORA = 384
KV_LORA = 256
MLA_WIDTH = MLA_HEADS * V_HEAD
ROPE_THETA = 10000.0
Q_BLOCK = 128
D_FF = 4 * D_MODEL
EPS = 1e-6
IN_SIZES = (SSM_WIDTH, Q_LORA, KV_LORA, QK_ROPE, 2 * D_MODEL)
IN_COLS = sum(IN_SIZES)
IN_SPLITS = [int(v) for v in np.cumsum(IN_SIZES)[:-1]]

kernel_name = "hybrid_s5_mla_encoder"


def rms_norm(x, g):
    xf = x.astype(jnp.float32)
    y = xf * lax.rsqrt(jnp.mean(xf * xf, axis=-1, keepdims=True) + EPS)
    return (y * g.astype(jnp.float32)).astype(x.dtype)


def rope_tables(seq, dtype):
    half = QK_ROPE // 2
    inv_freq = ROPE_THETA ** (-jnp.arange(half, dtype=jnp.float32) / half)
    ang = jnp.arange(seq, dtype=jnp.float32)[:, None] * inv_freq[None, :]
    return jnp.cos(ang).astype(dtype), jnp.sin(ang).astype(dtype)


def apply_rope(x, cos, sin):
    x1, x2 = jnp.split(x, 2, axis=-1)
    return jnp.concatenate([x1 * cos - x2 * sin, x1 * sin + x2 * cos], axis=-1)


def _affine_combine(e1, e2):
    a1r, a1i, b1r, b1i = e1
    a2r, a2i, b2r, b2i = e2
    ar = a2r * a1r - a2i * a1i
    ai = a2r * a1i + a2i * a1r
    br = a2r * b1r - a2i * b1i + b2r
    bi = a2r * b1i + a2i * b1r + b2i
    return ar, ai, br, bi


def zoh_discretise(lam_re, lam_im, log_step, b_re, b_im):
    f32 = jnp.float32
    lam_re = lam_re.astype(f32); lam_im = lam_im.astype(f32)
    step = jnp.exp(log_step.astype(f32))[:, None]
    mag = jnp.exp(lam_re * step)
    abar_r = mag * jnp.cos(lam_im * step)
    abar_i = mag * jnp.sin(lam_im * step)
    nr = abar_r - 1.0
    ni = abar_i
    den = lam_re * lam_re + lam_im * lam_im
    fr = (nr * lam_re + ni * lam_im) / den
    fi = (ni * lam_re - nr * lam_im) / den
    b_re = b_re.astype(f32); b_im = b_im.astype(f32)
    bbar_r = fr[..., None] * b_re - fi[..., None] * b_im
    bbar_i = fr[..., None] * b_im + fi[..., None] * b_re
    return abar_r, abar_i, bbar_r, bbar_i


def s5_states(u, lam_re, lam_im, log_step, b_re, b_im, reverse):
    abar_r, abar_i, bbar_r, bbar_i = zoh_discretise(lam_re, lam_im, log_step, b_re, b_im)
    bu_r = jnp.einsum('bsgp,gnp->bsgn', u, bbar_r)
    bu_i = jnp.einsum('bsgp,gnp->bsgn', u, bbar_i)
    a_r = jnp.broadcast_to(abar_r, bu_r.shape)
    a_i = jnp.broadcast_to(abar_i, bu_i.shape)
    _, _, xr, xi = lax.associative_scan(_affine_combine, (a_r, a_i, bu_r, bu_i), reverse=reverse, axis=1)
    return xr, xi


def s5_branch(u, lam_re, lam_im, log_step, b_re, b_im, c_re, c_im, d, w_glu, b_glu):
    bsz, seq, _ = u.shape
    uf = u.astype(jnp.float32).reshape(bsz, seq, SSM_GROUPS, SSM_GROUP)
    xr_f, xi_f = s5_states(uf, lam_re[0], lam_im[0], log_step[0], b_re[0], b_im[0], reverse=False)
    xr_b, xi_b = s5_states(uf, lam_re[1], lam_im[1], log_step[1], b_re[1], b_im[1], reverse=True)
    xr = xr_f + xr_b
    xi = xi_f + xi_b
    y = (jnp.einsum('bsgn,gpn->bsgp', xr, c_re.astype(jnp.float32))
         - jnp.einsum('bsgn,gpn->bsgp', xi, c_im.astype(jnp.float32))
         + d.astype(jnp.float32) * uf)
    y = y.reshape(bsz, seq, SSM_WIDTH).astype(u.dtype)
    y = jax.nn.gelu(y)
    return y * jax.nn.sigmoid(y @ w_glu + b_glu)


def mla_branch(cq, ckv, k_rope, q_norm_g, kv_norm_g, w_q_up, w_kv_up, q_head_g, k_head_g, cos, sin):
    bsz, seq, _ = cq.shape
    q = (rms_norm(cq, q_norm_g) @ w_q_up).reshape(bsz, seq, MLA_HEADS, QK_HEAD)
    kv = (rms_norm(ckv, kv_norm_g) @ w_kv_up).reshape(bsz, seq, MLA_HEADS, QK_NOPE + V_HEAD)
    k_nope, v = kv[..., :QK_NOPE], kv[..., QK_NOPE:]
    k = jnp.concatenate([k_nope, jnp.broadcast_to(k_rope[:, :, None, :], (bsz, seq, MLA_HEADS, QK_ROPE))], axis=-1)
    q = rms_norm(q, q_head_g)
    k = rms_norm(k, k_head_g)
    c4, s4 = cos[None, :, None, :], sin[None, :, None, :]
    q = jnp.concatenate([q[..., :QK_NOPE], apply_rope(q[..., QK_NOPE:], c4, s4)], axis=-1)
    k = jnp.concatenate([k[..., :QK_NOPE], apply_rope(k[..., QK_NOPE:], c4, s4)], axis=-1)
    q = q * (QK_HEAD ** -0.5)
    n_blk = seq // Q_BLOCK
    qb = q.reshape(bsz, n_blk, Q_BLOCK, MLA_HEADS, QK_HEAD).transpose(1, 0, 2, 3, 4)

    def attend(q_blk):
        s = jnp.einsum('bqhd,bkhd->bhqk', q_blk, k, preferred_element_type=jnp.float32)
        p = jax.nn.softmax(s, axis=-1)
        return jnp.einsum('bhqk,bkhd->bqhd', p.astype(v.dtype), v)

    o = lax.map(attend, qb)
    return o.transpose(1, 0, 2, 3, 4).reshape(bsz, seq, MLA_WIDTH)


def setup_inputs(seed: int = 0) -> dict:
    key = jax.random.key(seed)
    ks = jax.random.split(key, 32)
    f32 = jnp.float32
    G, N, P = SSM_GROUPS, SSM_STATE, SSM_GROUP

    def nrm(k, shape, scale):
        return jax.random.normal(k, shape, f32) * scale

    def gain(k, shape):
        return 1.0 + 0.05 * jax.random.normal(k, shape, f32)

    n_idx = jnp.arange(N, dtype=f32)
    lam_re = -0.5 + 0.01 * jax.random.normal(ks[4], (DEPTH, 2, G, N), f32)
    lam_im = math.pi * n_idx + 0.01 * jax.random.normal(ks[5], (DEPTH, 2, G, N), f32)
    log_step = jax.random.uniform(ks[6], (DEPTH, 2, G), f32, math.log(1e-3), math.log(1e-1))
    b_scale = (1.0 / math.sqrt(P)) / math.sqrt(2.0)
    c_scale = (1.0 / math.sqrt(N)) / math.sqrt(2.0)
    return {
        "x": jax.random.normal(ks[0], (BATCH, SEQ, D_MODEL), f32),
        "mix_norm_g": gain(ks[1], (DEPTH, D_MODEL)),
        "w_in": nrm(ks[2], (DEPTH, D_MODEL, IN_COLS), D_MODEL ** -0.5),
        "b_gate": nrm(ks[3], (DEPTH, 2, D_MODEL), 0.02),
        "ssm_lam_re": lam_re,
        "ssm_lam_im": lam_im,
        "ssm_log_step": log_step,
        "ssm_b_re": nrm(ks[7], (DEPTH, 2, G, N, P), b_scale),
        "ssm_b_im": nrm(ks[8], (DEPTH, 2, G, N, P), b_scale),
        "ssm_c_re": nrm(ks[9], (DEPTH, G, P, N), c_scale),
        "ssm_c_im": nrm(ks[10], (DEPTH, G, P, N), c_scale),
        "ssm_d": nrm(ks[11], (DEPTH, G, P), 1.0),
        "w_glu": nrm(ks[12], (DEPTH, SSM_WIDTH, SSM_WIDTH), SSM_WIDTH ** -0.5),
        "b_glu": nrm(ks[13], (DEPTH, SSM_WIDTH), 0.02),
        "w_out_ssm": nrm(ks[14], (DEPTH, SSM_WIDTH, D_MODEL), SSM_WIDTH ** -0.5),
        "q_norm_g": gain(ks[15], (DEPTH, Q_LORA)),
        "kv_norm_g": gain(ks[16], (DEPTH, KV_LORA)),
        "w_q_up": nrm(ks[17], (DEPTH, Q_LORA, MLA_HEADS * QK_HEAD), Q_LORA ** -0.5),
        "w_kv_up": nrm(ks[18], (DEPTH, KV_LORA, MLA_HEADS * (QK_NOPE + V_HEAD)), KV_LORA ** -0.5),
        "q_head_g": gain(ks[19], (DEPTH, QK_HEAD)),
        "k_head_g": gain(ks[20], (DEPTH, QK_HEAD)),
        "w_out_mla": nrm(ks[21], (DEPTH, MLA_WIDTH, D_MODEL), MLA_WIDTH ** -0.5),
        "w_o": nrm(ks[22], (DEPTH, D_MODEL, D_MODEL), D_MODEL ** -0.5),
        "ffn_norm_g": gain(ks[23], (DEPTH, D_MODEL)),
        "w_ff1": nrm(ks[24], (DEPTH, D_MODEL, D_FF), D_MODEL ** -0.5),
        "w_ff2": nrm(ks[25], (DEPTH, D_FF, D_MODEL), D_FF ** -0.5),
    }


def reference(x, mix_norm_g, w_in, b_gate, ssm_lam_re, ssm_lam_im, ssm_log_step, ssm_b_re, ssm_b_im,
              ssm_c_re, ssm_c_im, ssm_d, w_glu, b_glu, w_out_ssm, q_norm_g, kv_norm_g, w_q_up, w_kv_up,
              q_head_g, k_head_g, w_out_mla, w_o, ffn_norm_g, w_ff1, w_ff2):
    bsz, seq, _ = x.shape
    cos, sin = rope_tables(seq, x.dtype)
    for l in range(DEPTH):
        h = rms_norm(x, mix_norm_g[l])
        proj = h @ w_in[l]
        u_ssm, cq, ckv, k_rope, gate_pre = jnp.split(proj, IN_SPLITS, axis=-1)
        gates = jax.nn.sigmoid(gate_pre.reshape(bsz, seq, 2, D_MODEL) + b_gate[l])
        y_ssm = s5_branch(u_ssm, ssm_lam_re[l], ssm_lam_im[l], ssm_log_step[l], ssm_b_re[l], ssm_b_im[l],
                          ssm_c_re[l], ssm_c_im[l], ssm_d[l], w_glu[l], b_glu[l])
        y_mla = mla_branch(cq, ckv, k_rope, q_norm_g[l], kv_norm_g[l], w_q_up[l], w_kv_up[l],
                           q_head_g[l], k_head_g[l], cos, sin)
        merged = gates[:, :, 0, :] * (y_ssm @ w_out_ssm[l]) + gates[:, :, 1, :] * (y_mla @ w_out_mla[l])
        x = x + merged @ w_o[l]
        h = rms_norm(x, ffn_norm_g[l])
        x = x + jnp.square(jax.nn.relu(h @ w_ff1[l])) @ w_ff2[l]
    return x
```

```python
import functools
import math

import jax
import jax.numpy as jnp
from jax import lax
from jax.experimental import pallas as pl
from jax.experimental.pallas import tpu as pltpu

D_MODEL = 1024
SSM_WIDTH = 512
SSM_GROUP = 16
SSM_GROUPS = 32
SSM_STATE = 64
MLA_HEADS = 8
QK_NOPE = 64
QK_ROPE = 32
QK_HEAD = QK_NOPE + QK_ROPE
V_HEAD = 64
Q_LORA = 384
KV_LORA = 256
MLA_WIDTH = MLA_HEADS * V_HEAD
ROPE_THETA = 10000.0
D_FF = 4 * D_MODEL
EPS = 1e-6

LANE = 128
HEAD_PAD = LANE
CHUNK = 8
GROUPS_PER_BLK = LANE // SSM_GROUP
N_GBLK = SSM_GROUPS // GROUPS_PER_BLK
STATE_BLK = 2 * 2 * GROUPS_PER_BLK * SSM_STATE
SCAN_BLK = STATE_BLK // 4
VMEM_LIMIT = 56 * 1024 * 1024

_C_U = 0
_C_CQ = _C_U + SSM_WIDTH
_C_CKV = _C_CQ + Q_LORA
_C_GATE = _C_CKV + KV_LORA
_C_KR = _C_GATE + 2 * D_MODEL
_C_KRS = _C_KR + HEAD_PAD
IN_COLS_PAD = _C_KRS + HEAD_PAD

bf16 = jnp.bfloat16
f32 = jnp.float32


def _dot(a, b):
    return jnp.dot(a, b, preferred_element_type=f32)


def _rms(x, g):
    return x * lax.rsqrt(jnp.mean(x * x, axis=-1, keepdims=True) + EPS) * g


def _pre_kernel(x_ref, ng_ref, win_ref, bg_ref, qng_ref, kvng_ref, wq_ref, wkv_ref,
                hg_ref, cos_ref, sin_ref,
                u_ref, gates_ref, q_ref, k_ref, v_ref, *, tm):
    x = x_ref[0]
    h = _rms(x, ng_ref[...]).astype(bf16)
    proj = _dot(h, win_ref[...])
    u = proj[:, _C_U:_C_CQ]
    for gb in range(N_GBLK):
        u_ref[:, 0, gb] = u[:, gb * LANE:(gb + 1) * LANE].reshape(tm // CHUNK, CHUNK, LANE)
    gates_ref[0] = jax.nn.sigmoid(proj[:, _C_GATE:_C_KR] + bg_ref[...]).astype(bf16)

    cq = _rms(proj[:, _C_CQ:_C_CKV], qng_ref[...]).astype(bf16)
    ckv = _rms(proj[:, _C_CKV:_C_GATE], kvng_ref[...]).astype(bf16)
    qq = _dot(cq, wq_ref[...])
    kk = _dot(ckv, wkv_ref[...])
    kr = proj[:, _C_KR:_C_KRS]
    krs = proj[:, _C_KRS:IN_COLS_PAD]
    cos = cos_ref[...]
    sin = sin_ref[...]
    hg = hg_ref[...]
    qc, qs = hg[0:1] * cos, hg[1:2] * sin
    kc, ks = hg[2:3] * cos, hg[3:4] * sin
    ks_term = krs * ks
    one_col = hg[4:5]
    hw = MLA_HEADS * HEAD_PAD
    for hd in range(MLA_HEADS):
        sl = slice(hd * HEAD_PAD, (hd + 1) * HEAD_PAD)
        sl2 = slice(hw + hd * HEAD_PAD, hw + (hd + 1) * HEAD_PAD)
        qh = qq[:, sl]
        r = lax.rsqrt(jnp.sum(qh * qh, axis=-1, keepdims=True) * (1.0 / QK_HEAD) + EPS) * (QK_HEAD ** -0.5)
        q_ref[0, hd] = (r * (qh * qc + qq[:, sl2] * qs)).astype(bf16)
        kh = kk[:, sl] + kr
        r = lax.rsqrt(jnp.sum(kh * kh, axis=-1, keepdims=True) * (1.0 / QK_HEAD) + EPS)
        k_ref[0, hd] = (r * (kh * kc + ks_term)).astype(bf16)
        v_ref[0, hd] = (kk[:, sl2] + one_col).astype(bf16)


def _pre_call(x, ng, win, bg, qng, kvng, wq, wkv, hg, cos_t, sin_t, *, tm):
    bsz, seq, _ = x.shape
    nch = seq // CHUNK
    const = lambda shape: pl.BlockSpec(shape, lambda b, i: (0,) * len(shape))
    return pl.pallas_call(
        functools.partial(_pre_kernel, tm=tm),
        grid=(bsz, seq // tm),
        in_specs=[
            pl.BlockSpec((1, tm, D_MODEL), lambda b, i: (b, i, 0)),
            const((1, D_MODEL)),
            const((D_MODEL, IN_COLS_PAD)),
            const((1, 2 * D_MODEL)),
            const((1, Q_LORA)),
            const((1, KV_LORA)),
            const((Q_LORA, 2 * MLA_HEADS * HEAD_PAD)),
            const((KV_LORA, 2 * MLA_HEADS * HEAD_PAD)),
            const((8, HEAD_PAD)),
            pl.BlockSpec((tm, HEAD_PAD), lambda b, i: (i, 0)),
            pl.BlockSpec((tm, HEAD_PAD), lambda b, i: (i, 0)),
        ],
        out_specs=[
            pl.BlockSpec((tm // CHUNK, 1, N_GBLK, CHUNK, LANE), lambda b, i: (i, b, 0, 0, 0)),
            pl.BlockSpec((1, tm, 2 * D_MODEL), lambda b, i: (b, i, 0)),
            pl.BlockSpec((1, MLA_HEADS, tm, HEAD_PAD), lambda b, i: (b, 0, i, 0)),
            pl.BlockSpec((1, MLA_HEADS, tm, HEAD_PAD), lambda b, i: (b, 0, i, 0)),
            pl.BlockSpec((1, MLA_HEADS, tm, HEAD_PAD), lambda b, i: (b, 0, i, 0)),
        ],
        out_shape=[
            jax.ShapeDtypeStruct((nch, bsz, N_GBLK, CHUNK, LANE), f32),
            jax.ShapeDtypeStruct((bsz, seq, 2 * D_MODEL), bf16),
            jax.ShapeDtypeStruct((bsz, MLA_HEADS, seq, HEAD_PAD), bf16),
            jax.ShapeDtypeStruct((bsz, MLA_HEADS, seq, HEAD_PAD), bf16),
            jax.ShapeDtypeStruct((bsz, MLA_HEADS, seq, HEAD_PAD), bf16),
        ],
        compiler_params=pltpu.CompilerParams(
            dimension_semantics=("parallel", "parallel"), vmem_limit_bytes=VMEM_LIMIT),
        name="pre",
    )(x, ng, win, bg, qng, kvng, wq, wkv, hg, cos_t, sin_t)


def _attn_kernel(q_ref, k_ref, v_ref, o_ref, m_scr, acc_scr):
    kv = pl.program_id(3)

    @pl.when(kv == 0)
    def _():
        m_scr[...] = jnp.full(m_scr.shape, -jnp.inf, f32)
        acc_scr[...] = jnp.zeros(acc_scr.shape, f32)

    for hd in range(2):
        s = lax.dot_general(q_ref[0, hd], k_ref[0, hd], (((1,), (1,)), ((), ())),
                            preferred_element_type=f32)
        m_prev = m_scr[hd]
        m_new = jnp.maximum(m_prev, jnp.max(s, axis=-1, keepdims=True))
        alpha = jnp.exp(m_prev - m_new)
        p = jnp.exp(s - m_new).astype(bf16)
        acc_scr[hd] = alpha * acc_scr[hd] + _dot(p, v_ref[0, hd])
        m_scr[hd] = m_new

    @pl.when(kv == pl.num_programs(3) - 1)
    def _():
        outs = []
        for hd in range(2):
            acc = acc_scr[hd]
            outs.append(acc[:, :V_HEAD] / acc[:, V_HEAD:V_HEAD + 1])
        o_ref[0] = jnp.concatenate(outs, axis=-1).astype(o_ref.dtype)


def _attn_call(q, k, v, *, tq, tk):
    bsz, nh, seq, _ = q.shape
    return pl.pallas_call(
        _attn_kernel,
        grid=(bsz, nh // 2, seq // tq, seq // tk),
        in_specs=[
            pl.BlockSpec((1, 2, tq, HEAD_PAD), lambda b, h, i, j: (b, h, i, 0)),
            pl.BlockSpec((1, 2, tk, HEAD_PAD), lambda b, h, i, j: (b, h, j, 0)),
            pl.BlockSpec((1, 2, tk, HEAD_PAD), lambda b, h, i, j: (b, h, j, 0)),
        ],
        out_specs=pl.BlockSpec((1, tq, 2 * V_HEAD), lambda b, h, i, j: (b, i, h)),
        out_shape=jax.ShapeDtypeStruct((bsz, seq, MLA_WIDTH), bf16),
        scratch_shapes=[pltpu.VMEM((2, tq, 1), f32), pltpu.VMEM((2, tq, HEAD_PAD), f32)],
        compiler_params=pltpu.CompilerParams(
            dimension_semantics=("parallel", "parallel", "parallel", "arbitrary"),
            vmem_limit_bytes=VMEM_LIMIT),
        name="attn",
    )(q, k, v)


def _s5_state_kernel(u_ref, w_ref, s_ref):
    s_ref[...] = _dot(u_ref[...].astype(bf16), w_ref[0])


def _s5_scan_kernel(s_ref, ar_ref, ai_ref, h_ref, *, nch, nb):
    half = SCAN_BLK // 2
    rows = 2 * nb
    bwd = (pl.program_id(0) // 2) % 2
    ar = jnp.broadcast_to(ar_ref[...], (rows, half))
    ai = jnp.broadcast_to(ai_ref[...], (rows, half))
    low = lax.broadcasted_iota(jnp.int32, (rows, half), 0) < nb
    first = jnp.logical_xor(low, bwd == 1)

    def step(hr, hi, sr, si):
        return ar * hr - ai * hi + sr, ar * hi + ai * hr + si

    def swap(a):
        return pltpu.roll(a, nb, axis=0)

    def body(i, carry):
        hr, hi = carry
        t = jnp.where(bwd == 1, nch // 2 - 1 - i, i)
        row = pl.multiple_of(t * rows, rows)
        sr = s_ref[pl.ds(row, rows), 0:half]
        si = s_ref[pl.ds(row, rows), half:SCAN_BLK]
        mr, mi = step(hr, hi, sr, si)
        mr, mi = swap(mr), swap(mi)
        h_ref[pl.ds(row, rows), 0:half] = jnp.where(first, hr, mr)
        h_ref[pl.ds(row, rows), half:SCAN_BLK] = jnp.where(first, hi, mi)
        nr, ni = step(mr, mi, sr, si)
        return jnp.where(first, swap(nr), nr), jnp.where(first, swap(ni), ni)

    zero = jnp.zeros((rows, half), f32)
    lax.fori_loop(0, nch // 2, body, (zero, zero), unroll=4)


def _s5_out_kernel(u_ref, h_ref, m_ref, c_ref, y_ref):
    y_ref[...] = _dot(u_ref[...].astype(bf16), m_ref[0]) + _dot(h_ref[...].astype(bf16), c_ref[0])


def _s5_call(u_perm, tabs, *, rt):
    nch, bsz = u_perm.shape[0], u_perm.shape[1]
    kblk = CHUNK * LANE
    u2 = u_perm.reshape(nch * bsz, N_GBLK * kblk)
    m_w, b_w, c_w, a_re, a_im = tabs
    params = pltpu.CompilerParams(dimension_semantics=("parallel", "parallel"), vmem_limit_bytes=VMEM_LIMIT)
    states = pl.pallas_call(
        _s5_state_kernel,
        grid=(N_GBLK, nch // rt),
        in_specs=[
            pl.BlockSpec((rt * bsz, kblk), lambda g, i: (i, g)),
            pl.BlockSpec((1, kblk, STATE_BLK), lambda g, i: (g, 0, 0)),
        ],
        out_specs=pl.BlockSpec((rt * bsz, STATE_BLK), lambda g, i: (i, g)),
        out_shape=jax.ShapeDtypeStruct((nch * bsz, N_GBLK * STATE_BLK), f32),
        compiler_params=params,
        name="s5_state",
    )(u2, b_w)
    n_scan = N_GBLK * STATE_BLK // SCAN_BLK
    carried = pl.pallas_call(
        functools.partial(_s5_scan_kernel, nch=nch, nb=bsz),
        grid=(n_scan,),
        in_specs=[
            pl.BlockSpec((nch * bsz, SCAN_BLK), lambda j: (0, j)),
            pl.BlockSpec((1, SCAN_BLK // 2), lambda j: (0, j)),
            pl.BlockSpec((1, SCAN_BLK // 2), lambda j: (0, j)),
        ],
        out_specs=pl.BlockSpec((nch * bsz, SCAN_BLK), lambda j: (0, j)),
        out_shape=jax.ShapeDtypeStruct((nch * bsz, N_GBLK * STATE_BLK), f32),
        compiler_params=pltpu.CompilerParams(dimension_semantics=("parallel",), vmem_limit_bytes=VMEM_LIMIT),
        name="s5_scan",
    )(states, a_re, a_im)
    y = pl.pallas_call(
        _s5_out_kernel,
        grid=(N_GBLK, nch // rt),
        in_specs=[
            pl.BlockSpec((rt * bsz, kblk), lambda g, i: (i, g)),
            pl.BlockSpec((rt * bsz, STATE_BLK), lambda g, i: (i, g)),
            pl.BlockSpec((1, kblk, kblk), lambda g, i: (g, 0, 0)),
            pl.BlockSpec((1, STATE_BLK, kblk), lambda g, i: (g, 0, 0)),
        ],
        out_specs=pl.BlockSpec((rt * bsz, kblk), lambda g, i: (i, g)),
        out_shape=jax.ShapeDtypeStruct((nch * bsz, N_GBLK * kblk), f32),
        compiler_params=params,
        name="s5_out",
    )(u2, carried, m_w, c_w)
    return y.reshape(nch, bsz, N_GBLK, CHUNK, LANE)


def _s5_tables(lam_re, lam_im, log_step, b_re, b_im, c_re, c_im):
    L, G, N, P = CHUNK, SSM_GROUPS, SSM_STATE, SSM_GROUP
    hp = lax.Precision.HIGHEST
    step = jnp.exp(log_step)[..., None]
    zr, zi = lam_re * step, lam_im * step
    mag = jnp.exp(zr)
    abar_r, abar_i = mag * jnp.cos(zi), mag * jnp.sin(zi)
    nr, ni = abar_r - 1.0, abar_i
    den = lam_re * lam_re + lam_im * lam_im
    fr = (nr * lam_re + ni * lam_im) / den
    fi = (ni * lam_re - nr * lam_im) / den
    bbr = fr[..., None] * b_re - fi[..., None] * b_im
    bbi = fr[..., None] * b_im + fi[..., None] * b_re
    j = jnp.arange(L + 1, dtype=f32)[:, None, None, None]
    pmag = jnp.exp(j * zr)
    pr, pi = pmag * jnp.cos(j * zi), pmag * jnp.sin(j * zi)

    cpr = c_re[None, None] * pr[:L, :, :, None, :] - c_im[None, None] * pi[:L, :, :, None, :]
    cpi = c_re[None, None] * pi[:L, :, :, None, :] + c_im[None, None] * pr[:L, :, :, None, :]
    kern = (jnp.einsum('jdgpn,dgnq->jdgpq', cpr, bbr, precision=hp)
            - jnp.einsum('jdgpn,dgnq->jdgpq', cpi, bbi, precision=hp))
    s_idx = jnp.arange(L)[:, None]
    t_idx = jnp.arange(L)[None, :]
    kf = kern[:, 0][jnp.clip(t_idx - s_idx, 0, L - 1)] * (s_idx <= t_idx)[:, :, None, None, None]
    kb = kern[:, 1][jnp.clip(s_idx - t_idx, 0, L - 1)] * (s_idx >= t_idx)[:, :, None, None, None]
    mg = jnp.transpose(kf + kb, (2, 0, 4, 1, 3))
    eye = jnp.eye(GROUPS_PER_BLK, dtype=f32)
    mg = mg.reshape(N_GBLK, GROUPS_PER_BLK, L, P, L, P)
    m_w = jnp.einsum('Ggsqtp,gh->Gsgqthp', mg, eye).reshape(N_GBLK, L * LANE, L * LANE)

    eye3 = eye.reshape(GROUPS_PER_BLK, 2, GROUPS_PER_BLK // 2)
    psr = jnp.stack([pr[:L, 0][::-1], pr[:L, 1]])
    psi = jnp.stack([pi[:L, 0][::-1], pi[:L, 1]])
    wr = psr[..., None] * bbr[:, None] - psi[..., None] * bbi[:, None]
    wi = psr[..., None] * bbi[:, None] + psi[..., None] * bbr[:, None]
    wst = jnp.stack([wr, wi]).reshape(2, 2, L, N_GBLK, 2, GROUPS_PER_BLK // 2, N, P)
    b_w = jnp.einsum('pdsGhfnq,khf->Gskqdhpfn', wst, eye3).reshape(N_GBLK, L * LANE, STATE_BLK)

    pcr = jnp.stack([pr[1:, 0], pr[1:, 1][::-1]])
    pci = jnp.stack([pi[1:, 0], pi[1:, 1][::-1]])
    ccr = c_re[None, None] * pcr[:, :, :, None, :] - c_im[None, None] * pci[:, :, :, None, :]
    cci = c_re[None, None] * pci[:, :, :, None, :] + c_im[None, None] * pcr[:, :, :, None, :]
    cst = jnp.stack([ccr, -cci]).reshape(2, 2, L, N_GBLK, 2, GROUPS_PER_BLK // 2, P, N)
    c_w = jnp.einsum('pdtGhfjn,khf->Gdhpfntkj', cst, eye3).reshape(N_GBLK, STATE_BLK, L * LANE)

    def scan_order(a):
        a = a.reshape(2, N_GBLK, 2, GROUPS_PER_BLK // 2, N)
        return jnp.transpose(a, (1, 0, 2, 3, 4)).reshape(1, -1)

    return m_w.astype(bf16), b_w.astype(bf16), c_w.astype(bf16), scan_order(pr[L]), scan_order(pi[L])


def _mix_kernel(y_ref, u_ref, o_ref, gates_ref, x_ref, d_ref, wglu_ref, bglu_ref,
                wos_ref, wom_ref, wo_ref, out_ref, *, tm):
    parts = []
    for gb in range(N_GBLK):
        sl = slice(gb * LANE, (gb + 1) * LANE)
        yg = y_ref[:, 0, gb].reshape(tm, LANE)
        ug = u_ref[:, 0, gb].reshape(tm, LANE)
        parts.append(yg + d_ref[:, sl] * ug)
    y = jax.nn.gelu(jnp.concatenate(parts, axis=-1))
    y = y * jax.nn.sigmoid(_dot(y.astype(bf16), wglu_ref[...]) + bglu_ref[...])
    a = _dot(y.astype(bf16), wos_ref[...])
    m = _dot(o_ref[0], wom_ref[...])
    gates = gates_ref[0].astype(f32)
    merged = gates[:, :D_MODEL] * a + gates[:, D_MODEL:] * m
    out_ref[0] = x_ref[0] + _dot(merged.astype(bf16), wo_ref[...])


def _mix_call(y_perm, u_perm, o, gates, x, d, wglu, bglu, wos, wom, wo, *, tm):
    bsz, seq, _ = x.shape
    const = lambda shape: pl.BlockSpec(shape, lambda b, i: (0,) * len(shape))
    perm_spec = pl.BlockSpec((tm // CHUNK, 1, N_GBLK, CHUNK, LANE), lambda b, i: (i, b, 0, 0, 0))
    return pl.pallas_call(
        functools.partial(_mix_kernel, tm=tm),
        grid=(bsz, seq // tm),
        in_specs=[
            perm_spec, perm_spec,
            pl.BlockSpec((1, tm, MLA_WIDTH), lambda b, i: (b, i, 0)),
            pl.BlockSpec((1, tm, 2 * D_MODEL), lambda b, i: (b, i, 0)),
            pl.BlockSpec((1, tm, D_MODEL), lambda b, i: (b, i, 0)),
            const((1, SSM_WIDTH)),
            const((SSM_WIDTH, SSM_WIDTH)),
            const((1, SSM_WIDTH)),
            const((SSM_WIDTH, D_MODEL)),
            const((MLA_WIDTH, D_MODEL)),
            const((D_MODEL, D_MODEL)),
        ],
        out_specs=pl.BlockSpec((1, tm, D_MODEL), lambda b, i: (b, i, 0)),
        out_shape=jax.ShapeDtypeStruct(x.shape, f32),
        compiler_params=pltpu.CompilerParams(
            dimension_semantics=("parallel", "parallel"), vmem_limit_bytes=VMEM_LIMIT),
        name="mix",
    )(y_perm, u_perm, o, gates, x, d, wglu, bglu, wos, wom, wo)


def _ffn_kernel(x_ref, g_ref, w1_ref, w2_ref, out_ref, *, fb):
    x = x_ref[...]
    h = _rms(x, g_ref[...]).astype(bf16)
    acc = x
    for j in range(D_FF // fb):
        a = jnp.maximum(_dot(h, w1_ref[:, j * fb:(j + 1) * fb]), 0.0)
        acc = acc + _dot((a * a).astype(bf16), w2_ref[j * fb:(j + 1) * fb, :])
    out_ref[...] = acc


def _ffn_call(x2, g, w1, w2, *, tm, fb):
    ntok = x2.shape[0]
    return pl.pallas_call(
        functools.partial(_ffn_kernel, fb=fb),
        grid=(ntok // tm,),
        in_specs=[
            pl.BlockSpec((tm, D_MODEL), lambda i: (i, 0)),
            pl.BlockSpec((1, D_MODEL), lambda i: (0, 0)),
            pl.BlockSpec((D_MODEL, D_FF), lambda i: (0, 0)),
            pl.BlockSpec((D_FF, D_MODEL), lambda i: (0, 0)),
        ],
        out_specs=pl.BlockSpec((tm, D_MODEL), lambda i: (i, 0)),
        out_shape=jax.ShapeDtypeStruct(x2.shape, f32),
        compiler_params=pltpu.CompilerParams(dimension_semantics=("parallel",), vmem_limit_bytes=VMEM_LIMIT),
        name="ffn",
    )(x2, g, w1, w2)


def _rope_swap(a):
    half = QK_ROPE // 2
    return jnp.concatenate([a[..., half:], a[..., :half]], axis=-1)


def _place_rope(a):
    pad = [(0, 0)] * (a.ndim - 1) + [(QK_NOPE, HEAD_PAD - QK_HEAD)]
    return jnp.pad(a, pad)


def _pad_head(a, width):
    pad = [(0, 0)] * (a.ndim - 1) + [(0, HEAD_PAD - width)]
    a = jnp.pad(a, pad)
    return a.reshape(a.shape[:-2] + (MLA_HEADS * HEAD_PAD,))


def _layer_params(l, w_in, w_q_up, w_kv_up, q_head_g, k_head_g):
    w = w_in[l]
    o1, o2, o3, o4 = SSM_WIDTH, SSM_WIDTH + Q_LORA, SSM_WIDTH + Q_LORA + KV_LORA, SSM_WIDTH + Q_LORA + KV_LORA + QK_ROPE
    wkr = w[:, o3:o4]
    win = jnp.concatenate([w[:, :o3], w[:, o4:], _place_rope(wkr), _place_rope(_rope_swap(wkr))], axis=1).astype(bf16)
    wq = w_q_up[l].reshape(Q_LORA, MLA_HEADS, QK_HEAD)
    wq_sw = _place_rope(_rope_swap(wq[..., QK_NOPE:])).reshape(Q_LORA, MLA_HEADS * HEAD_PAD)
    wq = jnp.concatenate([_pad_head(wq, QK_HEAD), wq_sw], axis=1).astype(bf16)
    wkv = w_kv_up[l].reshape(KV_LORA, MLA_HEADS, QK_NOPE + V_HEAD)
    wkv = jnp.concatenate([_pad_head(wkv[..., :QK_NOPE], QK_NOPE), _pad_head(wkv[..., QK_NOPE:], V_HEAD)],
                          axis=1).astype(bf16)
    qg, kg = q_head_g[l], k_head_g[l]
    one_col = jnp.zeros((HEAD_PAD,), f32).at[V_HEAD].set(1.0)
    zero = jnp.zeros((HEAD_PAD,), f32)
    hg = jnp.stack([
        jnp.pad(qg, (0, HEAD_PAD - QK_HEAD)), _place_rope(_rope_swap(qg[QK_NOPE:])),
        jnp.pad(kg, (0, HEAD_PAD - QK_HEAD)), _place_rope(_rope_swap(kg[QK_NOPE:])),
        one_col, zero, zero, zero])
    return win, wq, wkv, hg


def _rope_tables(seq):
    half = QK_ROPE // 2
    inv_freq = ROPE_THETA ** (-jnp.arange(half, dtype=f32) / half)
    ang = jnp.arange(seq, dtype=f32)[:, None] * inv_freq[None, :]
    cos, sin = jnp.cos(ang), jnp.sin(ang)
    ones = jnp.ones((seq, QK_NOPE), f32)
    zpad = jnp.zeros((seq, HEAD_PAD - QK_HEAD), f32)
    cos_t = jnp.concatenate([ones, cos, cos, zpad], axis=1)
    sin_t = jnp.concatenate([0.0 * ones, -sin, sin, zpad], axis=1)
    return cos_t, sin_t


def kernel(x, mix_norm_g, w_in, b_gate, ssm_lam_re, ssm_lam_im, ssm_log_step, ssm_b_re, ssm_b_im, ssm_c_re, ssm_c_im, ssm_d, w_glu, b_glu, w_out_ssm, q_norm_g, kv_norm_g, w_q_up, w_kv_up, q_head_g, k_head_g, w_out_mla, w_o, ffn_norm_g, w_ff1, w_ff2):
    bsz, seq, _ = x.shape
    depth = w_in.shape[0]
    nch = seq // CHUNK
    tm = min(512, seq)
    tq = min(512, seq)
    tk = min(512, seq)
    rt = min(128, nch)
    cos_t, sin_t = _rope_tables(seq)
    for l in range(depth):
        win, wq, wkv, hg = _layer_params(l, w_in, w_q_up, w_kv_up, q_head_g, k_head_g)
        tabs = _s5_tables(ssm_lam_re[l], ssm_lam_im[l], ssm_log_step[l], ssm_b_re[l], ssm_b_im[l],
                          ssm_c_re[l], ssm_c_im[l])
        u_perm, gates, q, k, v = _pre_call(
            x, mix_norm_g[l][None], win, b_gate[l].reshape(1, 2 * D_MODEL), q_norm_g[l][None], kv_norm_g[l][None],
            wq, wkv, hg, cos_t, sin_t, tm=tm)
        o = _attn_call(q, k, v, tq=tq, tk=tk)
        y_perm = _s5_call(u_perm, tabs, rt=rt)
        x = _mix_call(y_perm, u_perm, o, gates, x, ssm_d[l].reshape(1, SSM_WIDTH), w_glu[l].astype(bf16),
                      b_glu[l][None], w_out_ssm[l].astype(bf16), w_out_mla[l].astype(bf16), w_o[l].astype(bf16), tm=tm)
        x = _ffn_call(x.reshape(bsz * seq, D_MODEL), ffn_norm_g[l][None], w_ff1[l].astype(bf16),
                      w_ff2[l].astype(bf16), tm=tm, fb=1024).reshape(bsz, seq, D_MODEL)
    return x
```

```python
import functools
import math

import jax
import jax.numpy as jnp
from jax import lax
from jax.experimental import pallas as pl
from jax.experimental.pallas import tpu as pltpu

D_MODEL = 1024
SSM_WIDTH = 512
SSM_GROUP = 16
SSM_GROUPS = 32
SSM_STATE = 64
MLA_HEADS = 8
QK_NOPE = 64
QK_ROPE = 32
QK_HEAD = QK_NOPE + QK_ROPE
V_HEAD = 64
Q_LORA = 384
KV_LORA = 256
MLA_WIDTH = MLA_HEADS * V_HEAD
ROPE_THETA = 10000.0
D_FF = 4 * D_MODEL
EPS = 1e-6

LANE = 128
HEAD_PAD = LANE
CHUNK = 8
GROUPS_PER_BLK = LANE // SSM_GROUP
N_GBLK = SSM_GROUPS // GROUPS_PER_BLK
STATE_BLK = 2 * 2 * GROUPS_PER_BLK * SSM_STATE
SCAN_BLK = STATE_BLK // 4
VMEM_LIMIT = 56 * 1024 * 1024
Q_SCALE = QK_HEAD ** -0.5 * math.log2(math.e)

_C_U = 0
_C_CQ = _C_U + SSM_WIDTH
_C_CKV = _C_CQ + Q_LORA
_C_GATE = _C_CKV + KV_LORA
_C_KR = _C_GATE + 2 * D_MODEL
_C_KRS = _C_KR + HEAD_PAD
IN_COLS_PAD = _C_KRS + HEAD_PAD

bf16 = jnp.bfloat16
f32 = jnp.float32


def _dot(a, b):
    return jnp.dot(a, b, preferred_element_type=f32)


def _rms(x, g):
    return x * lax.rsqrt(jnp.mean(x * x, axis=-1, keepdims=True) + EPS) * g


def _pre_kernel(x_ref, ng_ref, win_ref, bg_ref, qng_ref, kvng_ref, wq_ref, wkv_ref,
                hg_ref, cos_ref, sin_ref,
                u_ref, gates_ref, q_ref, k_ref, v_ref, *, tm):
    x = x_ref[0]
    h = _rms(x, ng_ref[...]).astype(bf16)
    proj = _dot(h, win_ref[...])
    u = proj[:, _C_U:_C_CQ]
    for gb in range(N_GBLK):
        u_ref[:, 0, gb] = u[:, gb * LANE:(gb + 1) * LANE].reshape(tm // CHUNK, CHUNK, LANE)
    gates_ref[0] = jax.nn.sigmoid(proj[:, _C_GATE:_C_KR] + bg_ref[...]).astype(bf16)

    cq = _rms(proj[:, _C_CQ:_C_CKV], qng_ref[...]).astype(bf16)
    ckv = _rms(proj[:, _C_CKV:_C_GATE], kvng_ref[...]).astype(bf16)
    qq = _dot(cq, wq_ref[...])
    kk = _dot(ckv, wkv_ref[...])
    kr = proj[:, _C_KR:_C_KRS]
    krs = proj[:, _C_KRS:IN_COLS_PAD]
    cos = cos_ref[...]
    sin = sin_ref[...]
    hg = hg_ref[...]
    qc, qs = hg[0:1] * cos, hg[1:2] * sin
    kc, ks = hg[2:3] * cos, hg[3:4] * sin
    ks_term = krs * ks
    one_col = hg[4:5]
    hw = MLA_HEADS * HEAD_PAD
    for hd in range(MLA_HEADS):
        sl = slice(hd * HEAD_PAD, (hd + 1) * HEAD_PAD)
        sl2 = slice(hw + hd * HEAD_PAD, hw + (hd + 1) * HEAD_PAD)
        qh = qq[:, sl]
        r = lax.rsqrt(jnp.sum(qh * qh, axis=-1, keepdims=True) * (1.0 / QK_HEAD) + EPS) * Q_SCALE
        q_ref[0, hd] = (r * (qh * qc + qq[:, sl2] * qs)).astype(bf16)
        kh = kk[:, sl] + kr
        r = lax.rsqrt(jnp.sum(kh * kh, axis=-1, keepdims=True) * (1.0 / QK_HEAD) + EPS)
        k_ref[0, hd] = (r * (kh * kc + ks_term)).astype(bf16)
        v_ref[0, hd] = (kk[:, sl2] + one_col).astype(bf16)


def _pre_call(x, ng, win, bg, qng, kvng, wq, wkv, hg, cos_t, sin_t, *, tm):
    bsz, seq, _ = x.shape
    nch = seq // CHUNK
    const = lambda shape: pl.BlockSpec(shape, lambda b, i: (0,) * len(shape))
    return pl.pallas_call(
        functools.partial(_pre_kernel, tm=tm),
        grid=(bsz, seq // tm),
        in_specs=[
            pl.BlockSpec((1, tm, D_MODEL), lambda b, i: (b, i, 0)),
            const((1, D_MODEL)),
            const((D_MODEL, IN_COLS_PAD)),
            const((1, 2 * D_MODEL)),
            const((1, Q_LORA)),
            const((1, KV_LORA)),
            const((Q_LORA, 2 * MLA_HEADS * HEAD_PAD)),
            const((KV_LORA, 2 * MLA_HEADS * HEAD_PAD)),
            const((8, HEAD_PAD)),
            pl.BlockSpec((tm, HEAD_PAD), lambda b, i: (i, 0)),
            pl.BlockSpec((tm, HEAD_PAD), lambda b, i: (i, 0)),
        ],
        out_specs=[
            pl.BlockSpec((tm // CHUNK, 1, N_GBLK, CHUNK, LANE), lambda b, i: (i, b, 0, 0, 0)),
            pl.BlockSpec((1, tm, 2 * D_MODEL), lambda b, i: (b, i, 0)),
            pl.BlockSpec((1, MLA_HEADS, tm, HEAD_PAD), lambda b, i: (b, 0, i, 0)),
            pl.BlockSpec((1, MLA_HEADS, tm, HEAD_PAD), lambda b, i: (b, 0, i, 0)),
            pl.BlockSpec((1, MLA_HEADS, tm, HEAD_PAD), lambda b, i: (b, 0, i, 0)),
        ],
        out_shape=[
            jax.ShapeDtypeStruct((nch, bsz, N_GBLK, CHUNK, LANE), f32),
            jax.ShapeDtypeStruct((bsz, seq, 2 * D_MODEL), bf16),
            jax.ShapeDtypeStruct((bsz, MLA_HEADS, seq, HEAD_PAD), bf16),
            jax.ShapeDtypeStruct((bsz, MLA_HEADS, seq, HEAD_PAD), bf16),
            jax.ShapeDtypeStruct((bsz, MLA_HEADS, seq, HEAD_PAD), bf16),
        ],
        compiler_params=pltpu.CompilerParams(
            dimension_semantics=("parallel", "parallel"), vmem_limit_bytes=VMEM_LIMIT),
        name="pre",
    )(x, ng, win, bg, qng, kvng, wq, wkv, hg, cos_t, sin_t)


def _attn_kernel(q_ref, k_ref, v_ref, o_ref, s_scr, m_scr, acc_scr, *, tk):
    tq = q_ref.shape[2]
    nk = k_ref.shape[2] // tk
    reps = tk // LANE
    outs = []
    for hd in range(2):
        q = q_ref[0, hd]
        m_scr[...] = jnp.full(m_scr.shape, -jnp.inf, f32)

        def qk_body(j, carry):
            kj = k_ref[0, hd, pl.ds(pl.multiple_of(j * tk, tk), tk), :]
            s = lax.dot_general(q, kj, (((1,), (1,)), ((), ())), preferred_element_type=f32)
            s_scr[j] = s
            m = m_scr[...]
            for c in range(reps):
                m = jnp.maximum(m, s[:, c * LANE:(c + 1) * LANE])
            m_scr[...] = m
            return carry

        lax.fori_loop(0, nk, qk_body, 0)
        m_scr[...] = jnp.broadcast_to(jnp.max(m_scr[...], axis=-1, keepdims=True), m_scr.shape)
        acc_scr[...] = jnp.zeros(acc_scr.shape, f32)

        def pv_body(j, carry):
            m = pltpu.repeat(m_scr[...], reps, axis=1)
            p = jnp.exp2(s_scr[j] - m).astype(bf16)
            vj = v_ref[0, hd, pl.ds(pl.multiple_of(j * tk, tk), tk), :]
            acc_scr[...] += _dot(p, vj)
            return carry

        lax.fori_loop(0, nk, pv_body, 0)
        acc = acc_scr[...]
        outs.append(acc[:, :V_HEAD] / acc[:, V_HEAD:V_HEAD + 1])
    o_ref[0] = jnp.concatenate(outs, axis=-1).astype(o_ref.dtype)


def _attn_call(q, k, v, *, tq, tk):
    bsz, nh, seq, _ = q.shape
    return pl.pallas_call(
        functools.partial(_attn_kernel, tk=tk),
        grid=(bsz, nh // 2, seq // tq),
        in_specs=[
            pl.BlockSpec((1, 2, tq, HEAD_PAD), lambda b, h, i: (b, h, i, 0)),
            pl.BlockSpec((1, 2, seq, HEAD_PAD), lambda b, h, i: (b, h, 0, 0)),
            pl.BlockSpec((1, 2, seq, HEAD_PAD), lambda b, h, i: (b, h, 0, 0)),
        ],
        out_specs=pl.BlockSpec((1, tq, 2 * V_HEAD), lambda b, h, i: (b, i, h)),
        out_shape=jax.ShapeDtypeStruct((bsz, seq, MLA_WIDTH), bf16),
        scratch_shapes=[pltpu.VMEM((seq // tk, tq, tk), f32), pltpu.VMEM((tq, LANE), f32),
                        pltpu.VMEM((tq, HEAD_PAD), f32)],
        compiler_params=pltpu.CompilerParams(
            dimension_semantics=("parallel", "parallel", "parallel"), vmem_limit_bytes=VMEM_LIMIT),
        name="attn",
    )(q, k, v)


def _s5_state_kernel(u_ref, w_ref, s_ref):
    s_ref[...] = _dot(u_ref[...].astype(bf16), w_ref[0])


def _s5_scan_kernel(s_ref, ar_ref, ai_ref, h_ref, *, nch, nb):
    half = SCAN_BLK // 2
    rows = 2 * nb
    bwd = (pl.program_id(0) // 2) % 2
    ar = jnp.broadcast_to(ar_ref[...], (rows, half))
    ai = jnp.broadcast_to(ai_ref[...], (rows, half))
    low = lax.broadcasted_iota(jnp.int32, (rows, half), 0) < nb
    first = jnp.logical_xor(low, bwd == 1)

    def step(hr, hi, sr, si):
        return ar * hr - ai * hi + sr, ar * hi + ai * hr + si

    def swap(a):
        return pltpu.roll(a, nb, axis=0)

    def body(i, carry):
        hr, hi = carry
        t = jnp.where(bwd == 1, nch // 2 - 1 - i, i)
        row = pl.multiple_of(t * rows, rows)
        sr = s_ref[pl.ds(row, rows), 0:half]
        si = s_ref[pl.ds(row, rows), half:SCAN_BLK]
        mr, mi = step(hr, hi, sr, si)
        mr, mi = swap(mr), swap(mi)
        h_ref[pl.ds(row, rows), 0:half] = jnp.where(first, hr, mr)
        h_ref[pl.ds(row, rows), half:SCAN_BLK] = jnp.where(first, hi, mi)
        nr, ni = step(mr, mi, sr, si)
        return jnp.where(first, swap(nr), nr), jnp.where(first, swap(ni), ni)

    zero = jnp.zeros((rows, half), f32)
    lax.fori_loop(0, nch // 2, body, (zero, zero), unroll=4)


def _s5_out_kernel(u_ref, h_ref, m_ref, c_ref, y_ref):
    y_ref[...] = _dot(u_ref[...].astype(bf16), m_ref[0]) + _dot(h_ref[...].astype(bf16), c_ref[0])


def _s5_call(u_perm, tabs, *, rt):
    nch, bsz = u_perm.shape[0], u_perm.shape[1]
    kblk = CHUNK * LANE
    u2 = u_perm.reshape(nch * bsz, N_GBLK * kblk)
    m_w, b_w, c_w, a_re, a_im = tabs
    params = pltpu.CompilerParams(dimension_semantics=("parallel", "parallel"), vmem_limit_bytes=VMEM_LIMIT)
    states = pl.pallas_call(
        _s5_state_kernel,
        grid=(N_GBLK, nch // rt),
        in_specs=[
            pl.BlockSpec((rt * bsz, kblk), lambda g, i: (i, g)),
            pl.BlockSpec((1, kblk, STATE_BLK), lambda g, i: (g, 0, 0)),
        ],
        out_specs=pl.BlockSpec((rt * bsz, STATE_BLK), lambda g, i: (i, g)),
        out_shape=jax.ShapeDtypeStruct((nch * bsz, N_GBLK * STATE_BLK), f32),
        compiler_params=params,
        name="s5_state",
    )(u2, b_w)
    n_scan = N_GBLK * STATE_BLK // SCAN_BLK
    carried = pl.pallas_call(
        functools.partial(_s5_scan_kernel, nch=nch, nb=bsz),
        grid=(n_scan,),
        in_specs=[
            pl.BlockSpec((nch * bsz, SCAN_BLK), lambda j: (0, j)),
            pl.BlockSpec((1, SCAN_BLK // 2), lambda j: (0, j)),
            pl.BlockSpec((1, SCAN_BLK // 2), lambda j: (0, j)),
        ],
        out_specs=pl.BlockSpec((nch * bsz, SCAN_BLK), lambda j: (0, j)),
        out_shape=jax.ShapeDtypeStruct((nch * bsz, N_GBLK * STATE_BLK), f32),
        compiler_params=pltpu.CompilerParams(dimension_semantics=("parallel",), vmem_limit_bytes=VMEM_LIMIT),
        name="s5_scan",
    )(states, a_re, a_im)
    y = pl.pallas_call(
        _s5_out_kernel,
        grid=(N_GBLK, nch // rt),
        in_specs=[
            pl.BlockSpec((rt * bsz, kblk), lambda g, i: (i, g)),
            pl.BlockSpec((rt * bsz, STATE_BLK), lambda g, i: (i, g)),
            pl.BlockSpec((1, kblk, kblk), lambda g, i: (g, 0, 0)),
            pl.BlockSpec((1, STATE_BLK, kblk), lambda g, i: (g, 0, 0)),
        ],
        out_specs=pl.BlockSpec((rt * bsz, kblk), lambda g, i: (i, g)),
        out_shape=jax.ShapeDtypeStruct((nch * bsz, N_GBLK * kblk), f32),
        compiler_params=params,
        name="s5_out",
    )(u2, carried, m_w, c_w)
    return y.reshape(nch, bsz, N_GBLK, CHUNK, LANE)


def _s5_tables(lam_re, lam_im, log_step, b_re, b_im, c_re, c_im):
    L, G, N, P = CHUNK, SSM_GROUPS, SSM_STATE, SSM_GROUP
    hp = lax.Precision.HIGHEST
    step = jnp.exp(log_step)[..., None]
    zr, zi = lam_re * step, lam_im * step
    mag = jnp.exp(zr)
    abar_r, abar_i = mag * jnp.cos(zi), mag * jnp.sin(zi)
    nr, ni = abar_r - 1.0, abar_i
    den = lam_re * lam_re + lam_im * lam_im
    fr = (nr * lam_re + ni * lam_im) / den
    fi = (ni * lam_re - nr * lam_im) / den
    bbr = fr[..., None] * b_re - fi[..., None] * b_im
    bbi = fr[..., None] * b_im + fi[..., None] * b_re
    j = jnp.arange(L + 1, dtype=f32)[:, None, None, None]
    pmag = jnp.exp(j * zr)
    pr, pi = pmag * jnp.cos(j * zi), pmag * jnp.sin(j * zi)

    cpr = c_re[None, None] * pr[:L, :, :, None, :] - c_im[None, None] * pi[:L, :, :, None, :]
    cpi = c_re[None, None] * pi[:L, :, :, None, :] + c_im[None, None] * pr[:L, :, :, None, :]
    kern = (jnp.einsum('jdgpn,dgnq->jdgpq', cpr, bbr, precision=hp)
            - jnp.einsum('jdgpn,dgnq->jdgpq', cpi, bbi, precision=hp))
    s_idx = jnp.arange(L)[:, None]
    t_idx = jnp.arange(L)[None, :]
    kf = kern[:, 0][jnp.clip(t_idx - s_idx, 0, L - 1)] * (s_idx <= t_idx)[:, :, None, None, None]
    kb = kern[:, 1][jnp.clip(s_idx - t_idx, 0, L - 1)] * (s_idx >= t_idx)[:, :, None, None, None]
    mg = jnp.transpose(kf + kb, (2, 0, 4, 1, 3))
    GB = GROUPS_PER_BLK
    mc = jnp.transpose(mg.reshape(N_GBLK, GB, L, P, L, P), (0, 2, 1, 3, 4, 5)).reshape(N_GBLK, L * LANE, L * P)

    psr = jnp.stack([pr[:L, 0][::-1], pr[:L, 1]])
    psi = jnp.stack([pi[:L, 0][::-1], pi[:L, 1]])
    wr = psr[..., None] * bbr[:, None] - psi[..., None] * bbi[:, None]
    wi = psr[..., None] * bbi[:, None] + psi[..., None] * bbr[:, None]
    wst = jnp.stack([wr, wi]).reshape(2, 2, L, N_GBLK, GB, N, P)
    bc = jnp.transpose(wst, (3, 2, 4, 6, 1, 0, 5)).reshape(N_GBLK, L * LANE, 4 * N)

    pcr = jnp.stack([pr[1:, 0], pr[1:, 1][::-1]])
    pci = jnp.stack([pi[1:, 0], pi[1:, 1][::-1]])
    ccr = c_re[None, None] * pcr[:, :, :, None, :] - c_im[None, None] * pci[:, :, :, None, :]
    cci = c_re[None, None] * pci[:, :, :, None, :] + c_im[None, None] * pcr[:, :, :, None, :]
    cst = jnp.stack([ccr, -cci]).reshape(2, 2, L, N_GBLK, 2, GB // 2, P, N)
    cc = jnp.transpose(cst, (3, 1, 4, 0, 5, 7, 2, 6)).reshape(N_GBLK, STATE_BLK, L * P)

    io_lane = jnp.arange(L * LANE)
    io_small = jnp.arange(L * P)
    st_lane = jnp.arange(STATE_BLK)
    st_small = jnp.arange(4 * N)
    spread_io = ((io_small[:, None] // P == io_lane[None, :] // LANE)
                 & (io_small[:, None] % P == io_lane[None, :] % P)).astype(f32)
    spread_st = ((st_small[:, None] // (2 * N) == st_lane[None, :] // (STATE_BLK // 2))
                 & ((st_small[:, None] // N) % 2 == (st_lane[None, :] // (4 * N)) % 2)
                 & (st_small[:, None] % N == st_lane[None, :] % N)).astype(f32)
    grp_io = (io_lane // P) % GB
    grp_st = ((st_lane // (STATE_BLK // 4)) % 2) * (GB // 2) + (st_lane // N) % (GB // 2)
    m_w = jnp.dot(mc, spread_io, precision=hp) * (grp_io[:, None] == grp_io[None, :])
    b_w = jnp.dot(bc, spread_st, precision=hp) * (grp_io[:, None] == grp_st[None, :])
    c_w = jnp.dot(cc, spread_io, precision=hp) * (grp_st[:, None] == grp_io[None, :])

    def scan_order(a):
        a = a.reshape(2, N_GBLK, 2, GB // 2, N)
        return jnp.transpose(a, (1, 0, 2, 3, 4)).reshape(1, -1)

    return m_w.astype(bf16), b_w.astype(bf16), c_w.astype(bf16), scan_order(pr[L]), scan_order(pi[L])


def _mix_kernel(y_ref, u_ref, o_ref, gates_ref, x_ref, d_ref, wglu_ref, bglu_ref,
                wos_ref, wom_ref, wo_ref, out_ref, *, tm):
    parts = []
    for gb in range(N_GBLK):
        sl = slice(gb * LANE, (gb + 1) * LANE)
        yg = y_ref[:, 0, gb].reshape(tm, LANE)
        ug = u_ref[:, 0, gb].reshape(tm, LANE)
        parts.append(yg + d_ref[:, sl] * ug)
    y = jax.nn.gelu(jnp.concatenate(parts, axis=-1))
    y = y * jax.nn.sigmoid(_dot(y.astype(bf16), wglu_ref[...]) + bglu_ref[...])
    a = _dot(y.astype(bf16), wos_ref[...])
    m = _dot(o_ref[0], wom_ref[...])
    gates = gates_ref[0].astype(f32)
    merged = gates[:, :D_MODEL] * a + gates[:, D_MODEL:] * m
    out_ref[0] = x_ref[0] + _dot(merged.astype(bf16), wo_ref[...])


def _mix_call(y_perm, u_perm, o, gates, x, d, wglu, bglu, wos, wom, wo, *, tm):
    bsz, seq, _ = x.shape
    const = lambda shape: pl.BlockSpec(shape, lambda b, i: (0,) * len(shape))
    perm_spec = pl.BlockSpec((tm // CHUNK, 1, N_GBLK, CHUNK, LANE), lambda b, i: (i, b, 0, 0, 0))
    return pl.pallas_call(
        functools.partial(_mix_kernel, tm=tm),
        grid=(bsz, seq // tm),
        in_specs=[
            perm_spec, perm_spec,
            pl.BlockSpec((1, tm, MLA_WIDTH), lambda b, i: (b, i, 0)),
            pl.BlockSpec((1, tm, 2 * D_MODEL), lambda b, i: (b, i, 0)),
            pl.BlockSpec((1, tm, D_MODEL), lambda b, i: (b, i, 0)),
            const((1, SSM_WIDTH)),
            const((SSM_WIDTH, SSM_WIDTH)),
            const((1, SSM_WIDTH)),
            const((SSM_WIDTH, D_MODEL)),
            const((MLA_WIDTH, D_MODEL)),
            const((D_MODEL, D_MODEL)),
        ],
        out_specs=pl.BlockSpec((1, tm, D_MODEL), lambda b, i: (b, i, 0)),
        out_shape=jax.ShapeDtypeStruct(x.shape, f32),
        compiler_params=pltpu.CompilerParams(
            dimension_semantics=("parallel", "parallel"), vmem_limit_bytes=VMEM_LIMIT),
        name="mix",
    )(y_perm, u_perm, o, gates, x, d, wglu, bglu, wos, wom, wo)


def _ffn_kernel(x_ref, g_ref, w1_ref, w2_ref, out_ref, *, fb):
    x = x_ref[...]
    h = _rms(x, g_ref[...]).astype(bf16)
    acc = x
    for j in range(D_FF // fb):
        a = jnp.maximum(_dot(h, w1_ref[:, j * fb:(j + 1) * fb]), 0.0)
        acc = acc + _dot((a * a).astype(bf16), w2_ref[j * fb:(j + 1) * fb, :])
    out_ref[...] = acc


def _ffn_call(x2, g, w1, w2, *, tm, fb):
    ntok = x2.shape[0]
    return pl.pallas_call(
        functools.partial(_ffn_kernel, fb=fb),
        grid=(ntok // tm,),
        in_specs=[
            pl.BlockSpec((tm, D_MODEL), lambda i: (i, 0)),
            pl.BlockSpec((1, D_MODEL), lambda i: (0, 0)),
            pl.BlockSpec((D_MODEL, D_FF), lambda i: (0, 0)),
            pl.BlockSpec((D_FF, D_MODEL), lambda i: (0, 0)),
        ],
        out_specs=pl.BlockSpec((tm, D_MODEL), lambda i: (i, 0)),
        out_shape=jax.ShapeDtypeStruct(x2.shape, f32),
        compiler_params=pltpu.CompilerParams(dimension_semantics=("parallel",), vmem_limit_bytes=VMEM_LIMIT),
        name="ffn",
    )(x2, g, w1, w2)


def _rope_swap(a):
    half = QK_ROPE // 2
    return jnp.concatenate([a[..., half:], a[..., :half]], axis=-1)


def _place_rope(a):
    pad = [(0, 0)] * (a.ndim - 1) + [(QK_NOPE, HEAD_PAD - QK_HEAD)]
    return jnp.pad(a, pad)


def _pad_head(a, width):
    pad = [(0, 0)] * (a.ndim - 1) + [(0, HEAD_PAD - width)]
    a = jnp.pad(a, pad)
    return a.reshape(a.shape[:-2] + (MLA_HEADS * HEAD_PAD,))


def _layer_params(w, w_q, w_kv, qg, kg):
    o3 = SSM_WIDTH + Q_LORA + KV_LORA
    o4 = o3 + QK_ROPE
    wkr = w[:, o3:o4]
    win = jnp.concatenate([w[:, :o3], w[:, o4:], _place_rope(wkr), _place_rope(_rope_swap(wkr))], axis=1).astype(bf16)
    wq = w_q.reshape(Q_LORA, MLA_HEADS, QK_HEAD)
    wq_sw = _place_rope(_rope_swap(wq[..., QK_NOPE:])).reshape(Q_LORA, MLA_HEADS * HEAD_PAD)
    wq = jnp.concatenate([_pad_head(wq, QK_HEAD), wq_sw], axis=1).astype(bf16)
    wkv = w_kv.reshape(KV_LORA, MLA_HEADS, QK_NOPE + V_HEAD)
    wkv = jnp.concatenate([_pad_head(wkv[..., :QK_NOPE], QK_NOPE), _pad_head(wkv[..., QK_NOPE:], V_HEAD)],
                          axis=1).astype(bf16)
    one_col = jnp.zeros((HEAD_PAD,), f32).at[V_HEAD].set(1.0)
    zero = jnp.zeros((HEAD_PAD,), f32)
    hg = jnp.stack([
        jnp.pad(qg, (0, HEAD_PAD - QK_HEAD)), _place_rope(_rope_swap(qg[QK_NOPE:])),
        jnp.pad(kg, (0, HEAD_PAD - QK_HEAD)), _place_rope(_rope_swap(kg[QK_NOPE:])),
        one_col, zero, zero, zero])
    return win, wq, wkv, hg


def _rope_tables(seq):
    half = QK_ROPE // 2
    inv_freq = ROPE_THETA ** (-jnp.arange(half, dtype=f32) / half)
    ang = jnp.arange(seq, dtype=f32)[:, None] * inv_freq[None, :]
    cos, sin = jnp.cos(ang), jnp.sin(ang)
    ones = jnp.ones((seq, QK_NOPE), f32)
    zpad = jnp.zeros((seq, HEAD_PAD - QK_HEAD), f32)
    cos_t = jnp.concatenate([ones, cos, cos, zpad], axis=1)
    sin_t = jnp.concatenate([0.0 * ones, -sin, sin, zpad], axis=1)
    return cos_t, sin_t


def kernel(x, mix_norm_g, w_in, b_gate, ssm_lam_re, ssm_lam_im, ssm_log_step, ssm_b_re, ssm_b_im, ssm_c_re, ssm_c_im, ssm_d, w_glu, b_glu, w_out_ssm, q_norm_g, kv_norm_g, w_q_up, w_kv_up, q_head_g, k_head_g, w_out_mla, w_o, ffn_norm_g, w_ff1, w_ff2):
    bsz, seq, _ = x.shape
    depth = w_in.shape[0]
    nch = seq // CHUNK
    tm = min(512, seq)
    tq = min(512, seq)
    tk = min(4096, seq)
    rt = min(128, nch)
    assert 2 * bsz == 8, "the S5 chunk scan packs two chunks of `bsz` batch rows into one 8-sublane tile"
    cos_t, sin_t = _rope_tables(seq)
    win, wq, wkv, hg = jax.vmap(_layer_params)(w_in, w_q_up, w_kv_up, q_head_g, k_head_g)
    tabs = jax.vmap(_s5_tables)(ssm_lam_re, ssm_lam_im, ssm_log_step, ssm_b_re, ssm_b_im, ssm_c_re, ssm_c_im)
    wglu, wos, wom, wo, w1, w2 = (a.astype(bf16) for a in (w_glu, w_out_ssm, w_out_mla, w_o, w_ff1, w_ff2))
    for l in range(depth):
        u_perm, gates, q, k, v = _pre_call(
            x, mix_norm_g[l][None], win[l], b_gate[l].reshape(1, 2 * D_MODEL), q_norm_g[l][None],
            kv_norm_g[l][None], wq[l], wkv[l], hg[l], cos_t, sin_t, tm=tm)
        o = _attn_call(q, k, v, tq=tq, tk=tk)
        y_perm = _s5_call(u_perm, tuple(t[l] for t in tabs), rt=rt)
        x = _mix_call(y_perm, u_perm, o, gates, x, ssm_d[l].reshape(1, SSM_WIDTH), wglu[l], b_glu[l][None],
                      wos[l], wom[l], wo[l], tm=tm)
        x = _ffn_call(x.reshape(bsz * seq, D_MODEL), ffn_norm_g[l][None], w1[l], w2[l],
                      tm=tm, fb=1024).reshape(bsz, seq, D_MODEL)
    return x
```

```python
import functools
import math

import jax
import jax.numpy as jnp
from jax import lax
from jax.experimental import pallas as pl
from jax.experimental.pallas import tpu as pltpu

D_MODEL = 1024
SSM_WIDTH = 512
SSM_GROUP = 16
SSM_GROUPS = 32
SSM_STATE = 64
MLA_HEADS = 8
QK_NOPE = 64
QK_ROPE = 32
QK_HEAD = QK_NOPE + QK_ROPE
V_HEAD = 64
Q_LORA = 384
KV_LORA = 256
MLA_WIDTH = MLA_HEADS * V_HEAD
ROPE_THETA = 10000.0
D_FF = 4 * D_MODEL
EPS = 1e-6

LANE = 128
HEAD_PAD = LANE
CHUNK = 8
GROUPS_PER_BLK = LANE // SSM_GROUP
N_GBLK = SSM_GROUPS // GROUPS_PER_BLK
STATE_BLK = 2 * 2 * GROUPS_PER_BLK * SSM_STATE
SCAN_BLK = STATE_BLK // 4
VMEM_LIMIT = 56 * 1024 * 1024
Q_SCALE = QK_HEAD ** -0.5 * math.log2(math.e)

_C_U = 0
_C_CQ = _C_U + SSM_WIDTH
_C_CKV = _C_CQ + Q_LORA
_C_GATE = _C_CKV + KV_LORA
_C_KR = _C_GATE + 2 * D_MODEL
_C_KRS = _C_KR + HEAD_PAD
IN_COLS_PAD = _C_KRS + HEAD_PAD

bf16 = jnp.bfloat16
f32 = jnp.float32


def _dot(a, b):
    return jnp.dot(a, b, preferred_element_type=f32)


def _rms(x, g):
    return x * lax.rsqrt(jnp.mean(x * x, axis=-1, keepdims=True) + EPS) * g


def _pre_kernel(x_ref, ng_ref, win_ref, bg_ref, qng_ref, kvng_ref, wq_ref, wkv_ref,
                hg_ref, cos_ref, sin_ref,
                u_ref, gates_ref, q_ref, k_ref, v_ref, *, tm):
    x = x_ref[0]
    h = _rms(x, ng_ref[...]).astype(bf16)
    proj = _dot(h, win_ref[...])
    u = proj[:, _C_U:_C_CQ]
    for gb in range(N_GBLK):
        u_ref[:, 0, gb] = u[:, gb * LANE:(gb + 1) * LANE].reshape(tm // CHUNK, CHUNK, LANE)
    gates_ref[0] = jax.nn.sigmoid(proj[:, _C_GATE:_C_KR] + bg_ref[...]).astype(bf16)

    cq = _rms(proj[:, _C_CQ:_C_CKV], qng_ref[...]).astype(bf16)
    ckv = _rms(proj[:, _C_CKV:_C_GATE], kvng_ref[...]).astype(bf16)
    qq = _dot(cq, wq_ref[...])
    kk = _dot(ckv, wkv_ref[...])
    kr = proj[:, _C_KR:_C_KRS]
    krs = proj[:, _C_KRS:IN_COLS_PAD]
    cos = cos_ref[...]
    sin = sin_ref[...]
    hg = hg_ref[...]
    qc, qs = hg[0:1] * cos, hg[1:2] * sin
    kc, ks = hg[2:3] * cos, hg[3:4] * sin
    ks_term = krs * ks
    one_col = hg[4:5]
    hw = MLA_HEADS * HEAD_PAD
    for hd in range(MLA_HEADS):
        sl = slice(hd * HEAD_PAD, (hd + 1) * HEAD_PAD)
        sl2 = slice(hw + hd * HEAD_PAD, hw + (hd + 1) * HEAD_PAD)
        qh = qq[:, sl]
        r = lax.rsqrt(jnp.sum(qh * qh, axis=-1, keepdims=True) * (1.0 / QK_HEAD) + EPS) * Q_SCALE
        q_ref[0, hd] = (r * (qh * qc + qq[:, sl2] * qs)).astype(bf16)
        kh = kk[:, sl] + kr
        r = lax.rsqrt(jnp.sum(kh * kh, axis=-1, keepdims=True) * (1.0 / QK_HEAD) + EPS)
        k_ref[0, hd] = (r * (kh * kc + ks_term)).astype(bf16)
        v_ref[0, hd] = (kk[:, sl2] + one_col).astype(bf16)


def _layer_spec(shape, l):
    return pl.BlockSpec((None,) + tuple(shape), lambda b, i: (l,) + (0,) * len(shape))


def _pre_call(x, ng, win, bg, qng, kvng, wq, wkv, hg, cos_t, sin_t, l, *, tm):
    bsz, seq, _ = x.shape
    nch = seq // CHUNK
    const = lambda shape: pl.BlockSpec(shape, lambda b, i: (0,) * len(shape))
    return pl.pallas_call(
        functools.partial(_pre_kernel, tm=tm),
        grid=(bsz, seq // tm),
        in_specs=[
            pl.BlockSpec((1, tm, D_MODEL), lambda b, i: (b, i, 0)),
            const((1, D_MODEL)),
            _layer_spec((D_MODEL, IN_COLS_PAD), l),
            const((1, 2 * D_MODEL)),
            const((1, Q_LORA)),
            const((1, KV_LORA)),
            _layer_spec((Q_LORA, 2 * MLA_HEADS * HEAD_PAD), l),
            _layer_spec((KV_LORA, 2 * MLA_HEADS * HEAD_PAD), l),
            _layer_spec((8, HEAD_PAD), l),
            pl.BlockSpec((tm, HEAD_PAD), lambda b, i: (i, 0)),
            pl.BlockSpec((tm, HEAD_PAD), lambda b, i: (i, 0)),
        ],
        out_specs=[
            pl.BlockSpec((tm // CHUNK, 1, N_GBLK, CHUNK, LANE), lambda b, i: (i, b, 0, 0, 0)),
            pl.BlockSpec((1, tm, 2 * D_MODEL), lambda b, i: (b, i, 0)),
            pl.BlockSpec((1, MLA_HEADS, tm, HEAD_PAD), lambda b, i: (b, 0, i, 0)),
            pl.BlockSpec((1, MLA_HEADS, tm, HEAD_PAD), lambda b, i: (b, 0, i, 0)),
            pl.BlockSpec((1, MLA_HEADS, tm, HEAD_PAD), lambda b, i: (b, 0, i, 0)),
        ],
        out_shape=[
            jax.ShapeDtypeStruct((nch, bsz, N_GBLK, CHUNK, LANE), f32),
            jax.ShapeDtypeStruct((bsz, seq, 2 * D_MODEL), bf16),
            jax.ShapeDtypeStruct((bsz, MLA_HEADS, seq, HEAD_PAD), bf16),
            jax.ShapeDtypeStruct((bsz, MLA_HEADS, seq, HEAD_PAD), bf16),
            jax.ShapeDtypeStruct((bsz, MLA_HEADS, seq, HEAD_PAD), bf16),
        ],
        compiler_params=pltpu.CompilerParams(
            dimension_semantics=("parallel", "parallel"), vmem_limit_bytes=VMEM_LIMIT),
        name="pre",
    )(x, ng, win, bg, qng, kvng, wq, wkv, hg, cos_t, sin_t)


def _attn_kernel(q_ref, k_ref, v_ref, o_ref, s_scr, m_scr, acc_scr, *, tk):
    n_heads = q_ref.shape[1]
    nk = k_ref.shape[2] // tk
    reps = tk // LANE

    def qk_chunk(hd, j):
        slot = hd % 2
        kj = k_ref[0, hd, j * tk:(j + 1) * tk, :]
        s = lax.dot_general(q_ref[0, hd], kj, (((1,), (1,)), ((), ())), preferred_element_type=f32)
        s_scr[slot, j] = s
        m = s[:, :LANE] if j == 0 else m_scr[slot]
        for c in range(1 if j == 0 else 0, reps):
            m = jnp.maximum(m, s[:, c * LANE:(c + 1) * LANE])
        m_scr[slot] = m

    def pv_chunk(hd, j):
        slot = hd % 2
        p = jnp.exp2(s_scr[slot, j] - pltpu.repeat(m_scr[slot], reps, axis=1)).astype(bf16)
        pv = _dot(p, v_ref[0, hd, j * tk:(j + 1) * tk, :])
        acc_scr[...] = pv if j == 0 else acc_scr[...] + pv

    outs = []
    for stage in range(n_heads + 1):
        if stage > 0:
            slot = (stage - 1) % 2
            m_scr[slot] = jnp.broadcast_to(jnp.max(m_scr[slot], axis=-1, keepdims=True), m_scr.shape[1:])
        for j in range(nk):
            if stage < n_heads:
                qk_chunk(stage, j)
            if stage > 0:
                pv_chunk(stage - 1, j)
        if stage > 0:
            acc = acc_scr[...]
            outs.append(acc[:, :V_HEAD] / acc[:, V_HEAD:V_HEAD + 1])
    o_ref[0] = jnp.concatenate(outs, axis=-1).astype(o_ref.dtype)


def _attn_call(q, k, v, *, tq, tk):
    bsz, nh, seq, _ = q.shape
    return pl.pallas_call(
        functools.partial(_attn_kernel, tk=tk),
        grid=(bsz, nh // 2, seq // tq),
        in_specs=[
            pl.BlockSpec((1, 2, tq, HEAD_PAD), lambda b, h, i: (b, h, i, 0)),
            pl.BlockSpec((1, 2, seq, HEAD_PAD), lambda b, h, i: (b, h, 0, 0)),
            pl.BlockSpec((1, 2, seq, HEAD_PAD), lambda b, h, i: (b, h, 0, 0)),
        ],
        out_specs=pl.BlockSpec((1, tq, 2 * V_HEAD), lambda b, h, i: (b, i, h)),
        out_shape=jax.ShapeDtypeStruct((bsz, seq, MLA_WIDTH), bf16),
        scratch_shapes=[pltpu.VMEM((2, seq // tk, tq, tk), f32), pltpu.VMEM((2, tq, LANE), f32),
                        pltpu.VMEM((tq, HEAD_PAD), f32)],
        compiler_params=pltpu.CompilerParams(
            dimension_semantics=("parallel", "parallel", "parallel"), vmem_limit_bytes=VMEM_LIMIT),
        name="attn",
    )(q, k, v)


def _spread(compact_ref, spread_ref, mask_ref):
    return (_dot(compact_ref[...], spread_ref[...]) * mask_ref[...]).astype(bf16)


def _s5_state_kernel(u_ref, bc_ref, spread_ref, mask_ref, s_ref, w_scr):
    @pl.when(pl.program_id(1) == 0)
    def _():
        w_scr[...] = _spread(bc_ref, spread_ref, mask_ref)

    s_ref[...] = _dot(u_ref[...].astype(bf16), w_scr[...])


def _s5_scan_kernel(s_ref, ar_ref, ai_ref, h_ref, *, nch, nb):
    half = SCAN_BLK // 2
    rows = 2 * nb
    bwd = (pl.program_id(0) // 2) % 2
    ar = jnp.broadcast_to(ar_ref[...], (rows, half))
    ai = jnp.broadcast_to(ai_ref[...], (rows, half))
    low = lax.broadcasted_iota(jnp.int32, (rows, half), 0) < nb
    first = jnp.logical_xor(low, bwd == 1)

    def step(hr, hi, sr, si):
        return ar * hr - ai * hi + sr, ar * hi + ai * hr + si

    def swap(a):
        return pltpu.roll(a, nb, axis=0)

    def body(i, carry):
        hr, hi = carry
        t = jnp.where(bwd == 1, nch // 2 - 1 - i, i)
        row = pl.multiple_of(t * rows, rows)
        sr = s_ref[pl.ds(row, rows), 0:half]
        si = s_ref[pl.ds(row, rows), half:SCAN_BLK]
        mr, mi = step(hr, hi, sr, si)
        mr, mi = swap(mr), swap(mi)
        h_ref[pl.ds(row, rows), 0:half] = jnp.where(first, hr, mr)
        h_ref[pl.ds(row, rows), half:SCAN_BLK] = jnp.where(first, hi, mi)
        nr, ni = step(mr, mi, sr, si)
        return jnp.where(first, swap(nr), nr), jnp.where(first, swap(ni), ni)

    zero = jnp.zeros((rows, half), f32)
    lax.fori_loop(0, nch // 2, body, (zero, zero), unroll=4)


def _s5_out_kernel(u_ref, h_ref, mc_ref, cc_ref, spread_ref, mask_m_ref, mask_c_ref, y_ref, m_scr, c_scr):
    @pl.when(pl.program_id(1) == 0)
    def _():
        m_scr[...] = _spread(mc_ref, spread_ref, mask_m_ref)
        c_scr[...] = _spread(cc_ref, spread_ref, mask_c_ref)

    y_ref[...] = _dot(u_ref[...].astype(bf16), m_scr[...]) + _dot(h_ref[...].astype(bf16), c_scr[...])


def _s5_spread_constants():
    L, P, N, GB = CHUNK, SSM_GROUP, SSM_STATE, GROUPS_PER_BLK
    io_lane = jnp.arange(L * LANE)
    io_small = jnp.arange(L * P)
    st_lane = jnp.arange(STATE_BLK)
    st_small = jnp.arange(4 * N)
    spread_io = ((io_small[:, None] // P == io_lane[None, :] // LANE)
                 & (io_small[:, None] % P == io_lane[None, :] % P))
    spread_st = ((st_small[:, None] // (2 * N) == st_lane[None, :] // (STATE_BLK // 2))
                 & ((st_small[:, None] // N) % 2 == (st_lane[None, :] // (4 * N)) % 2)
                 & (st_small[:, None] % N == st_lane[None, :] % N))
    grp_io = (io_lane // P) % GB
    grp_st = ((st_lane // (STATE_BLK // 4)) % 2) * (GB // 2) + (st_lane // N) % (GB // 2)
    mask_m = grp_io[:, None] == grp_io[None, :]
    mask_b = grp_io[:, None] == grp_st[None, :]
    mask_c = grp_st[:, None] == grp_io[None, :]
    return tuple(a.astype(bf16) for a in (spread_io, spread_st, mask_m, mask_b, mask_c))


def _s5_call(u_perm, tabs, consts, l, *, rt):
    nch, bsz = u_perm.shape[0], u_perm.shape[1]
    kblk = CHUNK * LANE
    u2 = u_perm.reshape(nch * bsz, N_GBLK * kblk)
    mc, bc, cc, a_re, a_im = tabs
    spread_io, spread_st, mask_m, mask_b, mask_c = consts
    small_io, small_st = CHUNK * SSM_GROUP, 4 * SSM_STATE
    params = pltpu.CompilerParams(dimension_semantics=("arbitrary", "arbitrary"), vmem_limit_bytes=VMEM_LIMIT)
    whole = lambda a: pl.BlockSpec(a.shape, lambda g, i: (0,) * a.ndim)
    states = pl.pallas_call(
        _s5_state_kernel,
        grid=(N_GBLK, nch // rt),
        in_specs=[
            pl.BlockSpec((rt * bsz, kblk), lambda g, i: (i, g)),
            pl.BlockSpec((None, None, kblk, small_st), lambda g, i: (l, g, 0, 0)),
            whole(spread_st), whole(mask_b),
        ],
        out_specs=pl.BlockSpec((rt * bsz, STATE_BLK), lambda g, i: (i, g)),
        out_shape=jax.ShapeDtypeStruct((nch * bsz, N_GBLK * STATE_BLK), f32),
        scratch_shapes=[pltpu.VMEM((kblk, STATE_BLK), bf16)],
        compiler_params=params,
        name="s5_state",
    )(u2, bc, spread_st, mask_b)
    n_scan = N_GBLK * STATE_BLK // SCAN_BLK
    carried = pl.pallas_call(
        functools.partial(_s5_scan_kernel, nch=nch, nb=bsz),
        grid=(n_scan,),
        in_specs=[
            pl.BlockSpec((nch * bsz, SCAN_BLK), lambda j: (0, j)),
            pl.BlockSpec((None, 1, SCAN_BLK // 2), lambda j: (l, 0, j)),
            pl.BlockSpec((None, 1, SCAN_BLK // 2), lambda j: (l, 0, j)),
        ],
        out_specs=pl.BlockSpec((nch * bsz, SCAN_BLK), lambda j: (0, j)),
        out_shape=jax.ShapeDtypeStruct((nch * bsz, N_GBLK * STATE_BLK), f32),
        compiler_params=pltpu.CompilerParams(dimension_semantics=("parallel",), vmem_limit_bytes=VMEM_LIMIT),
        name="s5_scan",
    )(states, a_re, a_im)
    y = pl.pallas_call(
        _s5_out_kernel,
        grid=(N_GBLK, nch // rt),
        in_specs=[
            pl.BlockSpec((rt * bsz, kblk), lambda g, i: (i, g)),
            pl.BlockSpec((rt * bsz, STATE_BLK), lambda g, i: (i, g)),
            pl.BlockSpec((None, None, kblk, small_io), lambda g, i: (l, g, 0, 0)),
            pl.BlockSpec((None, None, STATE_BLK, small_io), lambda g, i: (l, g, 0, 0)),
            whole(spread_io), whole(mask_m), whole(mask_c),
        ],
        out_specs=pl.BlockSpec((rt * bsz, kblk), lambda g, i: (i, g)),
        out_shape=jax.ShapeDtypeStruct((nch * bsz, N_GBLK * kblk), f32),
        scratch_shapes=[pltpu.VMEM((kblk, kblk), bf16), pltpu.VMEM((STATE_BLK, kblk), bf16)],
        compiler_params=params,
        name="s5_out",
    )(u2, carried, mc, cc, spread_io, mask_m, mask_c)
    return y.reshape(nch, bsz, N_GBLK, CHUNK, LANE)


def _s5_tables(lam_re, lam_im, log_step, b_re, b_im, c_re, c_im):
    L, G, N, P = CHUNK, SSM_GROUPS, SSM_STATE, SSM_GROUP
    hp = lax.Precision.HIGHEST
    step = jnp.exp(log_step)[..., None]
    zr, zi = lam_re * step, lam_im * step
    mag = jnp.exp(zr)
    abar_r, abar_i = mag * jnp.cos(zi), mag * jnp.sin(zi)
    nr, ni = abar_r - 1.0, abar_i
    den = lam_re * lam_re + lam_im * lam_im
    fr = (nr * lam_re + ni * lam_im) / den
    fi = (ni * lam_re - nr * lam_im) / den
    bbr = fr[..., None] * b_re - fi[..., None] * b_im
    bbi = fr[..., None] * b_im + fi[..., None] * b_re
    j = jnp.arange(L + 1, dtype=f32)[:, None, None, None]
    pmag = jnp.exp(j * zr)
    pr, pi = pmag * jnp.cos(j * zi), pmag * jnp.sin(j * zi)

    cpr = c_re[None, None] * pr[:L, :, :, None, :] - c_im[None, None] * pi[:L, :, :, None, :]
    cpi = c_re[None, None] * pi[:L, :, :, None, :] + c_im[None, None] * pr[:L, :, :, None, :]
    kern = (jnp.einsum('jdgpn,dgnq->jdgpq', cpr, bbr, precision=hp)
            - jnp.einsum('jdgpn,dgnq->jdgpq', cpi, bbi, precision=hp))
    s_idx = jnp.arange(L)[:, None]
    t_idx = jnp.arange(L)[None, :]
    kf = kern[:, 0][jnp.clip(t_idx - s_idx, 0, L - 1)] * (s_idx <= t_idx)[:, :, None, None, None]
    kb = kern[:, 1][jnp.clip(s_idx - t_idx, 0, L - 1)] * (s_idx >= t_idx)[:, :, None, None, None]
    mg = jnp.transpose(kf + kb, (2, 0, 4, 1, 3))
    GB = GROUPS_PER_BLK
    mc = jnp.transpose(mg.reshape(N_GBLK, GB, L, P, L, P), (0, 2, 1, 3, 4, 5)).reshape(N_GBLK, L * LANE, L * P)

    psr = jnp.stack([pr[:L, 0][::-1], pr[:L, 1]])
    psi = jnp.stack([pi[:L, 0][::-1], pi[:L, 1]])
    wr = psr[..., None] * bbr[:, None] - psi[..., None] * bbi[:, None]
    wi = psr[..., None] * bbi[:, None] + psi[..., None] * bbr[:, None]
    wst = jnp.stack([wr, wi]).reshape(2, 2, L, N_GBLK, GB, N, P)
    bc = jnp.transpose(wst, (3, 2, 4, 6, 1, 0, 5)).reshape(N_GBLK, L * LANE, 4 * N)

    pcr = jnp.stack([pr[1:, 0], pr[1:, 1][::-1]])
    pci = jnp.stack([pi[1:, 0], pi[1:, 1][::-1]])
    ccr = c_re[None, None] * pcr[:, :, :, None, :] - c_im[None, None] * pci[:, :, :, None, :]
    cci = c_re[None, None] * pci[:, :, :, None, :] + c_im[None, None] * pcr[:, :, :, None, :]
    cst = jnp.stack([ccr, -cci]).reshape(2, 2, L, N_GBLK, 2, GB // 2, P, N)
    cc = jnp.transpose(cst, (3, 1, 4, 0, 5, 7, 2, 6)).reshape(N_GBLK, STATE_BLK, L * P)

    def scan_order(a):
        a = a.reshape(2, N_GBLK, 2, GB // 2, N)
        return jnp.transpose(a, (1, 0, 2, 3, 4)).reshape(1, -1)

    return mc.astype(bf16), bc.astype(bf16), cc.astype(bf16), scan_order(pr[L]), scan_order(pi[L])


def _mix_kernel(y_ref, u_ref, o_ref, gates_ref, x_ref, d_ref, wglu_ref, bglu_ref,
                wos_ref, wom_ref, wo_ref, out_ref, *, tm):
    parts = []
    for gb in range(N_GBLK):
        sl = slice(gb * LANE, (gb + 1) * LANE)
        yg = y_ref[:, 0, gb].reshape(tm, LANE)
        ug = u_ref[:, 0, gb].reshape(tm, LANE)
        parts.append(yg + d_ref[:, sl] * ug)
    y = jax.nn.gelu(jnp.concatenate(parts, axis=-1))
    y = y * jax.nn.sigmoid(_dot(y.astype(bf16), wglu_ref[...]) + bglu_ref[...])
    a = _dot(y.astype(bf16), wos_ref[...])
    m = _dot(o_ref[0], wom_ref[...])
    gates = gates_ref[0].astype(f32)
    merged = gates[:, :D_MODEL] * a + gates[:, D_MODEL:] * m
    out_ref[0] = x_ref[0] + _dot(merged.astype(bf16), wo_ref[...])


def _mix_call(y_perm, u_perm, o, gates, x, d, wglu, bglu, wos, wom, wo, l, *, tm):
    bsz, seq, _ = x.shape
    const = lambda shape: pl.BlockSpec(shape, lambda b, i: (0,) * len(shape))
    perm_spec = pl.BlockSpec((tm // CHUNK, 1, N_GBLK, CHUNK, LANE), lambda b, i: (i, b, 0, 0, 0))
    return pl.pallas_call(
        functools.partial(_mix_kernel, tm=tm),
        grid=(bsz, seq // tm),
        in_specs=[
            perm_spec, perm_spec,
            pl.BlockSpec((1, tm, MLA_WIDTH), lambda b, i: (b, i, 0)),
            pl.BlockSpec((1, tm, 2 * D_MODEL), lambda b, i: (b, i, 0)),
            pl.BlockSpec((1, tm, D_MODEL), lambda b, i: (b, i, 0)),
            const((1, SSM_WIDTH)),
            _layer_spec((SSM_WIDTH, SSM_WIDTH), l),
            const((1, SSM_WIDTH)),
            _layer_spec((SSM_WIDTH, D_MODEL), l),
            _layer_spec((MLA_WIDTH, D_MODEL), l),
            _layer_spec((D_MODEL, D_MODEL), l),
        ],
        out_specs=pl.BlockSpec((1, tm, D_MODEL), lambda b, i: (b, i, 0)),
        out_shape=jax.ShapeDtypeStruct(x.shape, f32),
        compiler_params=pltpu.CompilerParams(
            dimension_semantics=("parallel", "parallel"), vmem_limit_bytes=VMEM_LIMIT),
        name="mix",
    )(y_perm, u_perm, o, gates, x, d, wglu, bglu, wos, wom, wo)


def _ffn_kernel(x_ref, g_ref, w1_ref, w2_ref, out_ref, *, fb):
    x = x_ref[...]
    h = _rms(x, g_ref[...]).astype(bf16)
    acc = x
    for j in range(D_FF // fb):
        a = jnp.maximum(_dot(h, w1_ref[:, j * fb:(j + 1) * fb]), 0.0)
        acc = acc + _dot((a * a).astype(bf16), w2_ref[j * fb:(j + 1) * fb, :])
    out_ref[...] = acc


def _ffn_call(x2, g, w1, w2, l, *, tm, fb):
    ntok = x2.shape[0]
    return pl.pallas_call(
        functools.partial(_ffn_kernel, fb=fb),
        grid=(ntok // tm,),
        in_specs=[
            pl.BlockSpec((tm, D_MODEL), lambda i: (i, 0)),
            pl.BlockSpec((1, D_MODEL), lambda i: (0, 0)),
            pl.BlockSpec((None, D_MODEL, D_FF), lambda i: (l, 0, 0)),
            pl.BlockSpec((None, D_FF, D_MODEL), lambda i: (l, 0, 0)),
        ],
        out_specs=pl.BlockSpec((tm, D_MODEL), lambda i: (i, 0)),
        out_shape=jax.ShapeDtypeStruct(x2.shape, f32),
        compiler_params=pltpu.CompilerParams(dimension_semantics=("parallel",), vmem_limit_bytes=VMEM_LIMIT),
        name="ffn",
    )(x2, g, w1, w2)


def _rope_swap(a):
    half = QK_ROPE // 2
    return jnp.concatenate([a[..., half:], a[..., :half]], axis=-1)


def _place_rope(a):
    pad = [(0, 0)] * (a.ndim - 1) + [(QK_NOPE, HEAD_PAD - QK_HEAD)]
    return jnp.pad(a, pad)


def _pad_head(a, width):
    pad = [(0, 0)] * (a.ndim - 1) + [(0, HEAD_PAD - width)]
    a = jnp.pad(a, pad)
    return a.reshape(a.shape[:-2] + (MLA_HEADS * HEAD_PAD,))


def _layer_params(w, w_q, w_kv, qg, kg):
    o3 = SSM_WIDTH + Q_LORA + KV_LORA
    o4 = o3 + QK_ROPE
    wkr = w[:, o3:o4]
    win = jnp.concatenate([w[:, :o3], w[:, o4:], _place_rope(wkr), _place_rope(_rope_swap(wkr))], axis=1).astype(bf16)
    wq = w_q.reshape(Q_LORA, MLA_HEADS, QK_HEAD)
    wq_sw = _place_rope(_rope_swap(wq[..., QK_NOPE:])).reshape(Q_LORA, MLA_HEADS * HEAD_PAD)
    wq = jnp.concatenate([_pad_head(wq, QK_HEAD), wq_sw], axis=1).astype(bf16)
    wkv = w_kv.reshape(KV_LORA, MLA_HEADS, QK_NOPE + V_HEAD)
    wkv = jnp.concatenate([_pad_head(wkv[..., :QK_NOPE], QK_NOPE), _pad_head(wkv[..., QK_NOPE:], V_HEAD)],
                          axis=1).astype(bf16)
    one_col = jnp.zeros((HEAD_PAD,), f32).at[V_HEAD].set(1.0)
    zero = jnp.zeros((HEAD_PAD,), f32)
    hg = jnp.stack([
        jnp.pad(qg, (0, HEAD_PAD - QK_HEAD)), _place_rope(_rope_swap(qg[QK_NOPE:])),
        jnp.pad(kg, (0, HEAD_PAD - QK_HEAD)), _place_rope(_rope_swap(kg[QK_NOPE:])),
        one_col, zero, zero, zero])
    return win, wq, wkv, hg


def _rope_tables(seq):
    half = QK_ROPE // 2
    inv_freq = ROPE_THETA ** (-jnp.arange(half, dtype=f32) / half)
    ang = jnp.arange(seq, dtype=f32)[:, None] * inv_freq[None, :]
    cos, sin = jnp.cos(ang), jnp.sin(ang)
    ones = jnp.ones((seq, QK_NOPE), f32)
    zpad = jnp.zeros((seq, HEAD_PAD - QK_HEAD), f32)
    cos_t = jnp.concatenate([ones, cos, cos, zpad], axis=1)
    sin_t = jnp.concatenate([0.0 * ones, -sin, sin, zpad], axis=1)
    return cos_t, sin_t


def kernel(x, mix_norm_g, w_in, b_gate, ssm_lam_re, ssm_lam_im, ssm_log_step, ssm_b_re, ssm_b_im, ssm_c_re, ssm_c_im, ssm_d, w_glu, b_glu, w_out_ssm, q_norm_g, kv_norm_g, w_q_up, w_kv_up, q_head_g, k_head_g, w_out_mla, w_o, ffn_norm_g, w_ff1, w_ff2):
    bsz, seq, _ = x.shape
    depth = w_in.shape[0]
    nch = seq // CHUNK
    tm = min(512, seq)
    tq = min(512, seq)
    tk = min(4096, seq)
    rt = min(128, nch)
    assert 2 * bsz == 8, "the S5 chunk scan packs two chunks of `bsz` batch rows into one 8-sublane tile"
    cos_t, sin_t = _rope_tables(seq)
    win, wq, wkv, hg = jax.vmap(_layer_params)(w_in, w_q_up, w_kv_up, q_head_g, k_head_g)
    tabs = jax.vmap(_s5_tables)(ssm_lam_re, ssm_lam_im, ssm_log_step, ssm_b_re, ssm_b_im, ssm_c_re, ssm_c_im)
    consts = _s5_spread_constants()
    wglu, wos, wom, wo, w1, w2 = (a.astype(bf16) for a in (w_glu, w_out_ssm, w_out_mla, w_o, w_ff1, w_ff2))
    for l in range(depth):
        u_perm, gates, q, k, v = _pre_call(
            x, mix_norm_g[l][None], win, b_gate[l].reshape(1, 2 * D_MODEL), q_norm_g[l][None],
            kv_norm_g[l][None], wq, wkv, hg, cos_t, sin_t, l, tm=tm)
        o = _attn_call(q, k, v, tq=tq, tk=tk)
        y_perm = _s5_call(u_perm, tabs, consts, l, rt=rt)
        x = _mix_call(y_perm, u_perm, o, gates, x, ssm_d[l].reshape(1, SSM_WIDTH), wglu, b_glu[l][None],
                      wos, wom, wo, l, tm=tm)
        x = _ffn_call(x.reshape(bsz * seq, D_MODEL), ffn_norm_g[l][None], w1, w2, l,
                      tm=tm, fb=1024).reshape(bsz, seq, D_MODEL)
    return x
```

```python
import functools
import math

import jax
import jax.numpy as jnp
from jax import lax
from jax.experimental import pallas as pl
from jax.experimental.pallas import tpu as pltpu

D_MODEL = 1024
SSM_WIDTH = 512
SSM_GROUP = 16
SSM_GROUPS = 32
SSM_STATE = 64
MLA_HEADS = 8
QK_NOPE = 64
QK_ROPE = 32
QK_HEAD = QK_NOPE + QK_ROPE
V_HEAD = 64
Q_LORA = 384
KV_LORA = 256
MLA_WIDTH = MLA_HEADS * V_HEAD
ROPE_THETA = 10000.0
D_FF = 4 * D_MODEL
EPS = 1e-6

LANE = 128
HEAD_PAD = LANE
CHUNK = 8
GROUPS_PER_BLK = LANE // SSM_GROUP
N_GBLK = SSM_GROUPS // GROUPS_PER_BLK
STATE_BLK = 2 * 2 * GROUPS_PER_BLK * SSM_STATE
SCAN_BLK = STATE_BLK // 4
VMEM_LIMIT = 56 * 1024 * 1024
Q_SCALE = QK_HEAD ** -0.5 * math.log2(math.e)

_C_U = 0
_C_CQ = _C_U + SSM_WIDTH
_C_CKV = _C_CQ + Q_LORA
_C_GATE = _C_CKV + KV_LORA
_C_KR = _C_GATE + 2 * D_MODEL
_C_KRS = _C_KR + HEAD_PAD
IN_COLS_PAD = _C_KRS + HEAD_PAD

bf16 = jnp.bfloat16
f32 = jnp.float32


def _dot(a, b):
    return jnp.dot(a, b, preferred_element_type=f32)


def _rms(x, g):
    return x * lax.rsqrt(jnp.mean(x * x, axis=-1, keepdims=True) + EPS) * g


def _pre_kernel(x_ref, ng_ref, win_ref, bg_ref, qng_ref, kvng_ref, wq_ref, wkv_ref,
                hg_ref, cos_ref, sin_ref,
                u_ref, gates_ref, q_ref, k_ref, v_ref, *, tm):
    x = x_ref[0]
    h = _rms(x, ng_ref[...]).astype(bf16)
    proj = _dot(h, win_ref[...])
    u = proj[:, _C_U:_C_CQ]
    for gb in range(N_GBLK):
        u_ref[:, 0, gb] = u[:, gb * LANE:(gb + 1) * LANE].reshape(tm // CHUNK, CHUNK, LANE)
    gates_ref[0] = jax.nn.sigmoid(proj[:, _C_GATE:_C_KR] + bg_ref[...]).astype(bf16)

    cq = _rms(proj[:, _C_CQ:_C_CKV], qng_ref[...]).astype(bf16)
    ckv = _rms(proj[:, _C_CKV:_C_GATE], kvng_ref[...]).astype(bf16)
    qq = _dot(cq, wq_ref[...])
    kk = _dot(ckv, wkv_ref[...])
    kr = proj[:, _C_KR:_C_KRS]
    krs = proj[:, _C_KRS:IN_COLS_PAD]
    cos = cos_ref[...]
    sin = sin_ref[...]
    hg = hg_ref[...]
    qc, qs = hg[0:1] * cos, hg[1:2] * sin
    kc, ks = hg[2:3] * cos, hg[3:4] * sin
    ks_term = krs * ks
    one_col = hg[4:5]
    hw = MLA_HEADS * HEAD_PAD
    for hd in range(MLA_HEADS):
        sl = slice(hd * HEAD_PAD, (hd + 1) * HEAD_PAD)
        sl2 = slice(hw + hd * HEAD_PAD, hw + (hd + 1) * HEAD_PAD)
        qh = qq[:, sl]
        r = lax.rsqrt(jnp.sum(qh * qh, axis=-1, keepdims=True) * (1.0 / QK_HEAD) + EPS) * Q_SCALE
        q_ref[0, hd] = (r * (qh * qc + qq[:, sl2] * qs)).astype(bf16)
        kh = kk[:, sl] + kr
        r = lax.rsqrt(jnp.sum(kh * kh, axis=-1, keepdims=True) * (1.0 / QK_HEAD) + EPS)
        k_ref[0, hd] = (r * (kh * kc + ks_term)).astype(bf16)
        v_ref[0, hd] = (kk[:, sl2] + one_col).astype(bf16)


def _layer_spec(shape, l):
    return pl.BlockSpec((None,) + tuple(shape), lambda b, i: (l,) + (0,) * len(shape))


def _pre_call(x, ng, win, bg, qng, kvng, wq, wkv, hg, cos_t, sin_t, l, *, tm):
    bsz, seq, _ = x.shape
    nch = seq // CHUNK
    const = lambda shape: pl.BlockSpec(shape, lambda b, i: (0,) * len(shape))
    return pl.pallas_call(
        functools.partial(_pre_kernel, tm=tm),
        grid=(bsz, seq // tm),
        in_specs=[
            pl.BlockSpec((1, tm, D_MODEL), lambda b, i: (b, i, 0)),
            const((1, D_MODEL)),
            _layer_spec((D_MODEL, IN_COLS_PAD), l),
            const((1, 2 * D_MODEL)),
            const((1, Q_LORA)),
            const((1, KV_LORA)),
            _layer_spec((Q_LORA, 2 * MLA_HEADS * HEAD_PAD), l),
            _layer_spec((KV_LORA, 2 * MLA_HEADS * HEAD_PAD), l),
            _layer_spec((8, HEAD_PAD), l),
            pl.BlockSpec((tm, HEAD_PAD), lambda b, i: (i, 0)),
            pl.BlockSpec((tm, HEAD_PAD), lambda b, i: (i, 0)),
        ],
        out_specs=[
            pl.BlockSpec((tm // CHUNK, 1, N_GBLK, CHUNK, LANE), lambda b, i: (i, b, 0, 0, 0)),
            pl.BlockSpec((1, tm, 2 * D_MODEL), lambda b, i: (b, i, 0)),
            pl.BlockSpec((1, MLA_HEADS, tm, HEAD_PAD), lambda b, i: (b, 0, i, 0)),
            pl.BlockSpec((1, MLA_HEADS, tm, HEAD_PAD), lambda b, i: (b, 0, i, 0)),
            pl.BlockSpec((1, MLA_HEADS, tm, HEAD_PAD), lambda b, i: (b, 0, i, 0)),
        ],
        out_shape=[
            jax.ShapeDtypeStruct((nch, bsz, N_GBLK, CHUNK, LANE), f32),
            jax.ShapeDtypeStruct((bsz, seq, 2 * D_MODEL), bf16),
            jax.ShapeDtypeStruct((bsz, MLA_HEADS, seq, HEAD_PAD), bf16),
            jax.ShapeDtypeStruct((bsz, MLA_HEADS, seq, HEAD_PAD), bf16),
            jax.ShapeDtypeStruct((bsz, MLA_HEADS, seq, HEAD_PAD), bf16),
        ],
        compiler_params=pltpu.CompilerParams(
            dimension_semantics=("parallel", "parallel"), vmem_limit_bytes=VMEM_LIMIT),
        name="pre",
    )(x, ng, win, bg, qng, kvng, wq, wkv, hg, cos_t, sin_t)


def _attn_kernel(q_ref, k_ref, v_ref, o_ref, s_scr, m_scr, acc_scr, *, tk):
    n_heads = q_ref.shape[1]
    nk = k_ref.shape[2] // tk
    reps = tk // LANE

    def qk_chunk(hd, j):
        slot = hd % 2
        kj = k_ref[0, hd, j * tk:(j + 1) * tk, :]
        s = lax.dot_general(q_ref[0, hd], kj, (((1,), (1,)), ((), ())), preferred_element_type=f32)
        s_scr[slot, j] = s
        m = s[:, :LANE] if j == 0 else m_scr[slot]
        for c in range(1 if j == 0 else 0, reps):
            m = jnp.maximum(m, s[:, c * LANE:(c + 1) * LANE])
        m_scr[slot] = m

    def pv_chunk(hd, j):
        slot = hd % 2
        p = jnp.exp2(s_scr[slot, j] - pltpu.repeat(m_scr[slot], reps, axis=1)).astype(bf16)
        pv = _dot(p, v_ref[0, hd, j * tk:(j + 1) * tk, :])
        acc_scr[...] = pv if j == 0 else acc_scr[...] + pv

    outs = []
    for stage in range(n_heads + 1):
        if stage > 0:
            slot = (stage - 1) % 2
            m_scr[slot] = jnp.broadcast_to(jnp.max(m_scr[slot], axis=-1, keepdims=True), m_scr.shape[1:])
        for j in range(nk):
            if stage < n_heads:
                qk_chunk(stage, j)
            if stage > 0:
                pv_chunk(stage - 1, j)
        if stage > 0:
            acc = acc_scr[...]
            outs.append(acc[:, :V_HEAD] / acc[:, V_HEAD:V_HEAD + 1])
    o_ref[0] = jnp.concatenate(outs, axis=-1).astype(o_ref.dtype)


def _attn_call(q, k, v, *, tq, tk):
    bsz, nh, seq, _ = q.shape
    return pl.pallas_call(
        functools.partial(_attn_kernel, tk=tk),
        grid=(bsz, nh // 2, seq // tq),
        in_specs=[
            pl.BlockSpec((1, 2, tq, HEAD_PAD), lambda b, h, i: (b, h, i, 0)),
            pl.BlockSpec((1, 2, seq, HEAD_PAD), lambda b, h, i: (b, h, 0, 0)),
            pl.BlockSpec((1, 2, seq, HEAD_PAD), lambda b, h, i: (b, h, 0, 0)),
        ],
        out_specs=pl.BlockSpec((1, tq, 2 * V_HEAD), lambda b, h, i: (b, i, h)),
        out_shape=jax.ShapeDtypeStruct((bsz, seq, MLA_WIDTH), bf16),
        scratch_shapes=[pltpu.VMEM((2, seq // tk, tq, tk), f32), pltpu.VMEM((2, tq, LANE), f32),
                        pltpu.VMEM((tq, HEAD_PAD), f32)],
        compiler_params=pltpu.CompilerParams(
            dimension_semantics=("parallel", "parallel", "parallel"), vmem_limit_bytes=VMEM_LIMIT),
        name="attn",
    )(q, k, v)


def _chunk_rows(u_ref):
    return jnp.concatenate([u_ref[:, t, :] for t in range(CHUNK)], axis=-1).astype(bf16)


def _spread(compact_ref, spread_ref, mask_ref):
    return (_dot(compact_ref[...], spread_ref[...]) * mask_ref[...]).astype(bf16)


def _s5_state_kernel(u_ref, bc_ref, spread_ref, mask_ref, s_ref, w_scr):
    @pl.when(pl.program_id(1) == 0)
    def _():
        w_scr[...] = _spread(bc_ref, spread_ref, mask_ref)

    s_ref[...] = _dot(_chunk_rows(u_ref), w_scr[...])


def _s5_scan_kernel(s_ref, ar_ref, ai_ref, h_ref, *, nch, nb):
    half = SCAN_BLK // 2
    rows = 2 * nb
    bwd = (pl.program_id(0) // 2) % 2
    ar = jnp.broadcast_to(ar_ref[...], (rows, half))
    ai = jnp.broadcast_to(ai_ref[...], (rows, half))
    a2r, a2i = ar * ar - ai * ai, 2.0 * ar * ai
    low = lax.broadcasted_iota(jnp.int32, (rows, half), 0) < nb
    first = jnp.logical_xor(low, bwd == 1)

    def swap(a):
        return pltpu.roll(a, nb, axis=0)

    def body(i, carry):
        hr, hi = carry
        t = jnp.where(bwd == 1, nch // 2 - 1 - i, i)
        row = pl.multiple_of(t * rows, rows)
        sr = s_ref[pl.ds(row, rows), 0:half]
        si = s_ref[pl.ds(row, rows), half:SCAN_BLK]
        xr, xi = swap(sr), swap(si)
        fr, fi = jnp.where(first, sr, xr), jnp.where(first, si, xi)
        gr, gi = jnp.where(first, xr, sr), jnp.where(first, xi, si)
        mr, mi = ar * hr - ai * hi + fr, ar * hi + ai * hr + fi
        h_ref[pl.ds(row, rows), 0:half] = jnp.where(first, hr, mr)
        h_ref[pl.ds(row, rows), half:SCAN_BLK] = jnp.where(first, hi, mi)
        tr, ti = ar * fr - ai * fi + gr, ar * fi + ai * fr + gi
        return a2r * hr - a2i * hi + tr, a2r * hi + a2i * hr + ti

    zero = jnp.zeros((rows, half), f32)
    lax.fori_loop(0, nch // 2, body, (zero, zero), unroll=4)


def _s5_out_kernel(u_ref, h_ref, mc_ref, cc_ref, spread_ref, mask_m_ref, mask_c_ref, y_ref, m_scr, c_scr):
    @pl.when(pl.program_id(1) == 0)
    def _():
        m_scr[...] = _spread(mc_ref, spread_ref, mask_m_ref)
        c_scr[...] = _spread(cc_ref, spread_ref, mask_c_ref)

    y = _dot(_chunk_rows(u_ref), m_scr[...]) + _dot(h_ref[...].astype(bf16), c_scr[...])
    for t in range(CHUNK):
        y_ref[:, t, :] = y[:, t * LANE:(t + 1) * LANE]


def _s5_spread_constants():
    L, P, N, GB = CHUNK, SSM_GROUP, SSM_STATE, GROUPS_PER_BLK
    io_lane = jnp.arange(L * LANE)
    io_small = jnp.arange(L * P)
    st_lane = jnp.arange(STATE_BLK)
    st_small = jnp.arange(4 * N)
    spread_io = ((io_small[:, None] // P == io_lane[None, :] // LANE)
                 & (io_small[:, None] % P == io_lane[None, :] % P))
    spread_st = ((st_small[:, None] // (2 * N) == st_lane[None, :] // (STATE_BLK // 2))
                 & ((st_small[:, None] // N) % 2 == (st_lane[None, :] // (4 * N)) % 2)
                 & (st_small[:, None] % N == st_lane[None, :] % N))
    grp_io = (io_lane // P) % GB
    grp_st = ((st_lane // (STATE_BLK // 4)) % 2) * (GB // 2) + (st_lane // N) % (GB // 2)
    mask_m = grp_io[:, None] == grp_io[None, :]
    mask_b = grp_io[:, None] == grp_st[None, :]
    mask_c = grp_st[:, None] == grp_io[None, :]
    return tuple(a.astype(bf16) for a in (spread_io, spread_st, mask_m, mask_b, mask_c))


def _s5_call(u_perm, tabs, consts, l, *, rt):
    nch, bsz = u_perm.shape[0], u_perm.shape[1]
    kblk = CHUNK * LANE
    u2 = u_perm.reshape(nch * bsz, N_GBLK, CHUNK, LANE)
    io_spec = pl.BlockSpec((rt * bsz, None, CHUNK, LANE), lambda g, i: (i, g, 0, 0))
    mc, bc, cc, a_re, a_im = tabs
    spread_io, spread_st, mask_m, mask_b, mask_c = consts
    small_io, small_st = CHUNK * SSM_GROUP, 4 * SSM_STATE
    params = pltpu.CompilerParams(dimension_semantics=("arbitrary", "arbitrary"), vmem_limit_bytes=VMEM_LIMIT)
    whole = lambda a: pl.BlockSpec(a.shape, lambda g, i: (0,) * a.ndim)
    states = pl.pallas_call(
        _s5_state_kernel,
        grid=(N_GBLK, nch // rt),
        in_specs=[
            io_spec,
            pl.BlockSpec((None, None, kblk, small_st), lambda g, i: (l, g, 0, 0)),
            whole(spread_st), whole(mask_b),
        ],
        out_specs=pl.BlockSpec((rt * bsz, STATE_BLK), lambda g, i: (i, g)),
        out_shape=jax.ShapeDtypeStruct((nch * bsz, N_GBLK * STATE_BLK), f32),
        scratch_shapes=[pltpu.VMEM((kblk, STATE_BLK), bf16)],
        compiler_params=params,
        name="s5_state",
    )(u2, bc, spread_st, mask_b)
    n_scan = N_GBLK * STATE_BLK // SCAN_BLK
    carried = pl.pallas_call(
        functools.partial(_s5_scan_kernel, nch=nch, nb=bsz),
        grid=(n_scan,),
        in_specs=[
            pl.BlockSpec((nch * bsz, SCAN_BLK), lambda j: (0, j)),
            pl.BlockSpec((None, 1, SCAN_BLK // 2), lambda j: (l, 0, j)),
            pl.BlockSpec((None, 1, SCAN_BLK // 2), lambda j: (l, 0, j)),
        ],
        out_specs=pl.BlockSpec((nch * bsz, SCAN_BLK), lambda j: (0, j)),
        out_shape=jax.ShapeDtypeStruct((nch * bsz, N_GBLK * STATE_BLK), f32),
        compiler_params=pltpu.CompilerParams(dimension_semantics=("parallel",), vmem_limit_bytes=VMEM_LIMIT),
        name="s5_scan",
    )(states, a_re, a_im)
    y = pl.pallas_call(
        _s5_out_kernel,
        grid=(N_GBLK, nch // rt),
        in_specs=[
            io_spec,
            pl.BlockSpec((rt * bsz, STATE_BLK), lambda g, i: (i, g)),
            pl.BlockSpec((None, None, kblk, small_io), lambda g, i: (l, g, 0, 0)),
            pl.BlockSpec((None, None, STATE_BLK, small_io), lambda g, i: (l, g, 0, 0)),
            whole(spread_io), whole(mask_m), whole(mask_c),
        ],
        out_specs=io_spec,
        out_shape=jax.ShapeDtypeStruct((nch * bsz, N_GBLK, CHUNK, LANE), f32),
        scratch_shapes=[pltpu.VMEM((kblk, kblk), bf16), pltpu.VMEM((STATE_BLK, kblk), bf16)],
        compiler_params=params,
        name="s5_out",
    )(u2, carried, mc, cc, spread_io, mask_m, mask_c)
    return y.reshape(nch, bsz, N_GBLK, CHUNK, LANE)


def _s5_tables(lam_re, lam_im, log_step, b_re, b_im, c_re, c_im):
    L, G, N, P = CHUNK, SSM_GROUPS, SSM_STATE, SSM_GROUP
    hp = lax.Precision.HIGH
    step = jnp.exp(log_step)[..., None]
    zr, zi = lam_re * step, lam_im * step
    mag = jnp.exp(zr)
    abar_r, abar_i = mag * jnp.cos(zi), mag * jnp.sin(zi)
    nr, ni = abar_r - 1.0, abar_i
    den = lam_re * lam_re + lam_im * lam_im
    fr = (nr * lam_re + ni * lam_im) / den
    fi = (ni * lam_re - nr * lam_im) / den
    bbr = fr[..., None] * b_re - fi[..., None] * b_im
    bbi = fr[..., None] * b_im + fi[..., None] * b_re
    j = jnp.arange(L + 1, dtype=f32)[:, None, None, None]
    pmag = jnp.exp(j * zr)
    pr, pi = pmag * jnp.cos(j * zi), pmag * jnp.sin(j * zi)

    cpr = c_re[None, None] * pr[:L, :, :, None, :] - c_im[None, None] * pi[:L, :, :, None, :]
    cpi = c_re[None, None] * pi[:L, :, :, None, :] + c_im[None, None] * pr[:L, :, :, None, :]
    kern = (jnp.einsum('jdgpn,dgnq->jdgpq', cpr, bbr, precision=hp)
            - jnp.einsum('jdgpn,dgnq->jdgpq', cpi, bbi, precision=hp))
    s_idx = jnp.arange(L)[:, None]
    t_idx = jnp.arange(L)[None, :]
    kf = kern[:, 0][jnp.clip(t_idx - s_idx, 0, L - 1)] * (s_idx <= t_idx)[:, :, None, None, None]
    kb = kern[:, 1][jnp.clip(s_idx - t_idx, 0, L - 1)] * (s_idx >= t_idx)[:, :, None, None, None]
    mg = jnp.transpose(kf + kb, (2, 0, 4, 1, 3))
    GB = GROUPS_PER_BLK
    mc = jnp.transpose(mg.reshape(N_GBLK, GB, L, P, L, P), (0, 2, 1, 3, 4, 5)).reshape(N_GBLK, L * LANE, L * P)

    psr = jnp.stack([pr[:L, 0][::-1], pr[:L, 1]])
    psi = jnp.stack([pi[:L, 0][::-1], pi[:L, 1]])
    wr = psr[..., None] * bbr[:, None] - psi[..., None] * bbi[:, None]
    wi = psr[..., None] * bbi[:, None] + psi[..., None] * bbr[:, None]
    wst = jnp.stack([wr, wi]).reshape(2, 2, L, N_GBLK, GB, N, P)
    bc = jnp.transpose(wst, (3, 2, 4, 6, 1, 0, 5)).reshape(N_GBLK, L * LANE, 4 * N)

    pcr = jnp.stack([pr[1:, 0], pr[1:, 1][::-1]])
    pci = jnp.stack([pi[1:, 0], pi[1:, 1][::-1]])
    ccr = c_re[None, None] * pcr[:, :, :, None, :] - c_im[None, None] * pci[:, :, :, None, :]
    cci = c_re[None, None] * pci[:, :, :, None, :] + c_im[None, None] * pcr[:, :, :, None, :]
    cst = jnp.stack([ccr, -cci]).reshape(2, 2, L, N_GBLK, 2, GB // 2, P, N)
    cc = jnp.transpose(cst, (3, 1, 4, 0, 5, 7, 2, 6)).reshape(N_GBLK, STATE_BLK, L * P)

    def scan_order(a):
        a = a.reshape(2, N_GBLK, 2, GB // 2, N)
        return jnp.transpose(a, (1, 0, 2, 3, 4)).reshape(1, -1)

    return mc.astype(bf16), bc.astype(bf16), cc.astype(bf16), scan_order(pr[L]), scan_order(pi[L])


def _mix_kernel(y_ref, u_ref, o_ref, gates_ref, x_ref, d_ref, wglu_ref, bglu_ref,
                wos_ref, wom_ref, wo_ref, out_ref, *, tm):
    parts = []
    for gb in range(N_GBLK):
        sl = slice(gb * LANE, (gb + 1) * LANE)
        yg = y_ref[:, 0, gb].reshape(tm, LANE)
        ug = u_ref[:, 0, gb].reshape(tm, LANE)
        parts.append(yg + d_ref[:, sl] * ug)
    y = jax.nn.gelu(jnp.concatenate(parts, axis=-1))
    y = y * jax.nn.sigmoid(_dot(y.astype(bf16), wglu_ref[...]) + bglu_ref[...])
    a = _dot(y.astype(bf16), wos_ref[...])
    m = _dot(o_ref[0], wom_ref[...])
    gates = gates_ref[0].astype(f32)
    merged = gates[:, :D_MODEL] * a + gates[:, D_MODEL:] * m
    out_ref[0] = x_ref[0] + _dot(merged.astype(bf16), wo_ref[...])


def _mix_call(y_perm, u_perm, o, gates, x, d, wglu, bglu, wos, wom, wo, l, *, tm):
    bsz, seq, _ = x.shape
    const = lambda shape: pl.BlockSpec(shape, lambda b, i: (0,) * len(shape))
    perm_spec = pl.BlockSpec((tm // CHUNK, 1, N_GBLK, CHUNK, LANE), lambda b, i: (i, b, 0, 0, 0))
    return pl.pallas_call(
        functools.partial(_mix_kernel, tm=tm),
        grid=(bsz, seq // tm),
        in_specs=[
            perm_spec, perm_spec,
            pl.BlockSpec((1, tm, MLA_WIDTH), lambda b, i: (b, i, 0)),
            pl.BlockSpec((1, tm, 2 * D_MODEL), lambda b, i: (b, i, 0)),
            pl.BlockSpec((1, tm, D_MODEL), lambda b, i: (b, i, 0)),
            const((1, SSM_WIDTH)),
            _layer_spec((SSM_WIDTH, SSM_WIDTH), l),
            const((1, SSM_WIDTH)),
            _layer_spec((SSM_WIDTH, D_MODEL), l),
            _layer_spec((MLA_WIDTH, D_MODEL), l),
            _layer_spec((D_MODEL, D_MODEL), l),
        ],
        out_specs=pl.BlockSpec((1, tm, D_MODEL), lambda b, i: (b, i, 0)),
        out_shape=jax.ShapeDtypeStruct(x.shape, f32),
        compiler_params=pltpu.CompilerParams(
            dimension_semantics=("parallel", "parallel"), vmem_limit_bytes=VMEM_LIMIT),
        name="mix",
    )(y_perm, u_perm, o, gates, x, d, wglu, bglu, wos, wom, wo)


def _ffn_kernel(x_ref, g_ref, w1_ref, w2_ref, out_ref, *, fb):
    x = x_ref[...]
    h = _rms(x, g_ref[...]).astype(bf16)
    acc = x
    for j in range(D_FF // fb):
        a = jnp.maximum(_dot(h, w1_ref[:, j * fb:(j + 1) * fb]), 0.0)
        acc = acc + _dot((a * a).astype(bf16), w2_ref[j * fb:(j + 1) * fb, :])
    out_ref[...] = acc


def _ffn_call(x2, g, w1, w2, l, *, tm, fb):
    ntok = x2.shape[0]
    return pl.pallas_call(
        functools.partial(_ffn_kernel, fb=fb),
        grid=(ntok // tm,),
        in_specs=[
            pl.BlockSpec((tm, D_MODEL), lambda i: (i, 0)),
            pl.BlockSpec((1, D_MODEL), lambda i: (0, 0)),
            pl.BlockSpec((None, D_MODEL, D_FF), lambda i: (l, 0, 0)),
            pl.BlockSpec((None, D_FF, D_MODEL), lambda i: (l, 0, 0)),
        ],
        out_specs=pl.BlockSpec((tm, D_MODEL), lambda i: (i, 0)),
        out_shape=jax.ShapeDtypeStruct(x2.shape, f32),
        compiler_params=pltpu.CompilerParams(dimension_semantics=("parallel",), vmem_limit_bytes=VMEM_LIMIT),
        name="ffn",
    )(x2, g, w1, w2)


def _rope_swap(a):
    half = QK_ROPE // 2
    return jnp.concatenate([a[..., half:], a[..., :half]], axis=-1)


def _place_rope(a):
    pad = [(0, 0)] * (a.ndim - 1) + [(QK_NOPE, HEAD_PAD - QK_HEAD)]
    return jnp.pad(a, pad)


def _pad_head(a, width):
    pad = [(0, 0)] * (a.ndim - 1) + [(0, HEAD_PAD - width)]
    a = jnp.pad(a, pad)
    return a.reshape(a.shape[:-2] + (MLA_HEADS * HEAD_PAD,))


def _layer_params(w, w_q, w_kv, qg, kg):
    o3 = SSM_WIDTH + Q_LORA + KV_LORA
    o4 = o3 + QK_ROPE
    wkr = w[:, o3:o4]
    win = jnp.concatenate([w[:, :o3], w[:, o4:], _place_rope(wkr), _place_rope(_rope_swap(wkr))], axis=1).astype(bf16)
    wq = w_q.reshape(Q_LORA, MLA_HEADS, QK_HEAD)
    wq_sw = _place_rope(_rope_swap(wq[..., QK_NOPE:])).reshape(Q_LORA, MLA_HEADS * HEAD_PAD)
    wq = jnp.concatenate([_pad_head(wq, QK_HEAD), wq_sw], axis=1).astype(bf16)
    wkv = w_kv.reshape(KV_LORA, MLA_HEADS, QK_NOPE + V_HEAD)
    wkv = jnp.concatenate([_pad_head(wkv[..., :QK_NOPE], QK_NOPE), _pad_head(wkv[..., QK_NOPE:], V_HEAD)],
                          axis=1).astype(bf16)
    one_col = jnp.zeros((HEAD_PAD,), f32).at[V_HEAD].set(1.0)
    zero = jnp.zeros((HEAD_PAD,), f32)
    hg = jnp.stack([
        jnp.pad(qg, (0, HEAD_PAD - QK_HEAD)), _place_rope(_rope_swap(qg[QK_NOPE:])),
        jnp.pad(kg, (0, HEAD_PAD - QK_HEAD)), _place_rope(_rope_swap(kg[QK_NOPE:])),
        one_col, zero, zero, zero])
    return win, wq, wkv, hg


def _rope_tables(seq):
    half = QK_ROPE // 2
    inv_freq = ROPE_THETA ** (-jnp.arange(half, dtype=f32) / half)
    ang = jnp.arange(seq, dtype=f32)[:, None] * inv_freq[None, :]
    cos, sin = jnp.cos(ang), jnp.sin(ang)
    ones = jnp.ones((seq, QK_NOPE), f32)
    zpad = jnp.zeros((seq, HEAD_PAD - QK_HEAD), f32)
    cos_t = jnp.concatenate([ones, cos, cos, zpad], axis=1)
    sin_t = jnp.concatenate([0.0 * ones, -sin, sin, zpad], axis=1)
    return cos_t, sin_t


def kernel(x, mix_norm_g, w_in, b_gate, ssm_lam_re, ssm_lam_im, ssm_log_step, ssm_b_re, ssm_b_im, ssm_c_re, ssm_c_im, ssm_d, w_glu, b_glu, w_out_ssm, q_norm_g, kv_norm_g, w_q_up, w_kv_up, q_head_g, k_head_g, w_out_mla, w_o, ffn_norm_g, w_ff1, w_ff2):
    bsz, seq, _ = x.shape
    depth = w_in.shape[0]
    nch = seq // CHUNK
    tm = min(512, seq)
    tq = min(512, seq)
    tk = min(4096, seq)
    rt = min(128, nch)
    assert 2 * bsz == 8, "the S5 chunk scan packs two chunks of `bsz` batch rows into one 8-sublane tile"
    cos_t, sin_t = _rope_tables(seq)
    win, wq, wkv, hg = jax.vmap(_layer_params)(w_in, w_q_up, w_kv_up, q_head_g, k_head_g)
    tabs = jax.vmap(_s5_tables)(ssm_lam_re, ssm_lam_im, ssm_log_step, ssm_b_re, ssm_b_im, ssm_c_re, ssm_c_im)
    consts = _s5_spread_constants()
    wglu, wos, wom, wo, w1, w2 = (a.astype(bf16) for a in (w_glu, w_out_ssm, w_out_mla, w_o, w_ff1, w_ff2))
    for l in range(depth):
        u_perm, gates, q, k, v = _pre_call(
            x, mix_norm_g[l][None], win, b_gate[l].reshape(1, 2 * D_MODEL), q_norm_g[l][None],
            kv_norm_g[l][None], wq, wkv, hg, cos_t, sin_t, l, tm=tm)
        o = _attn_call(q, k, v, tq=tq, tk=tk)
        y_perm = _s5_call(u_perm, tabs, consts, l, rt=rt)
        x = _mix_call(y_perm, u_perm, o, gates, x, ssm_d[l].reshape(1, SSM_WIDTH), wglu, b_glu[l][None],
                      wos, wom, wo, l, tm=tm)
        x = _ffn_call(x.reshape(bsz * seq, D_MODEL), ffn_norm_g[l][None], w1, w2, l,
                      tm=tm, fb=1024).reshape(bsz, seq, D_MODEL)
    return x
```

```python
import functools
import math

import jax
import jax.numpy as jnp
from jax import lax
from jax.experimental import pallas as pl
from jax.experimental.pallas import tpu as pltpu

D_MODEL = 1024
SSM_WIDTH = 512
SSM_GROUP = 16
SSM_GROUPS = 32
SSM_STATE = 64
MLA_HEADS = 8
QK_NOPE = 64
QK_ROPE = 32
QK_HEAD = QK_NOPE + QK_ROPE
V_HEAD = 64
Q_LORA = 384
KV_LORA = 256
MLA_WIDTH = MLA_HEADS * V_HEAD
ROPE_THETA = 10000.0
D_FF = 4 * D_MODEL
EPS = 1e-6

LANE = 128
HEAD_PAD = LANE
CHUNK = 8
GROUPS_PER_BLK = LANE // SSM_GROUP
N_GBLK = SSM_GROUPS // GROUPS_PER_BLK
STATE_BLK = 2 * 2 * GROUPS_PER_BLK * SSM_STATE
SCAN_BLK = STATE_BLK // 4
VMEM_LIMIT = 56 * 1024 * 1024
Q_SCALE = QK_HEAD ** -0.5 * math.log2(math.e)

_C_U = 0
_C_CQ = _C_U + SSM_WIDTH
_C_CKV = _C_CQ + Q_LORA
_C_GATE = _C_CKV + KV_LORA
_C_KR = _C_GATE + 2 * D_MODEL
_C_KRS = _C_KR + HEAD_PAD
IN_COLS_PAD = _C_KRS + HEAD_PAD

bf16 = jnp.bfloat16
f32 = jnp.float32


def _dot(a, b):
    return jnp.dot(a, b, preferred_element_type=f32)


def _rms(x, g):
    return x * lax.rsqrt(jnp.mean(x * x, axis=-1, keepdims=True) + EPS) * g


def _pre_kernel(x_ref, ng_ref, win_ref, bg_ref, qng_ref, kvng_ref, wq_ref, wkv_ref,
                hg_ref, cos_ref, sin_ref,
                u_ref, gates_ref, q_ref, k_ref, v_ref, *, tm):
    x = x_ref[0]
    h = _rms(x, ng_ref[...]).astype(bf16)
    proj = _dot(h, win_ref[...])
    u = proj[:, _C_U:_C_CQ]
    for gb in range(N_GBLK):
        u_ref[:, 0, gb] = u[:, gb * LANE:(gb + 1) * LANE].reshape(tm // CHUNK, CHUNK, LANE)
    gates_ref[0] = jax.nn.sigmoid(proj[:, _C_GATE:_C_KR] + bg_ref[...]).astype(bf16)

    cq = _rms(proj[:, _C_CQ:_C_CKV], qng_ref[...]).astype(bf16)
    ckv = _rms(proj[:, _C_CKV:_C_GATE], kvng_ref[...]).astype(bf16)
    qq = _dot(cq, wq_ref[...])
    kk = _dot(ckv, wkv_ref[...])
    kr = proj[:, _C_KR:_C_KRS]
    krs = proj[:, _C_KRS:IN_COLS_PAD]
    cos = cos_ref[...]
    sin = sin_ref[...]
    hg = hg_ref[...]
    qc, qs = hg[0:1] * cos, hg[1:2] * sin
    kc, ks = hg[2:3] * cos, hg[3:4] * sin
    ks_term = krs * ks
    one_col = hg[4:5]
    hw = MLA_HEADS * HEAD_PAD
    for hd in range(MLA_HEADS):
        sl = slice(hd * HEAD_PAD, (hd + 1) * HEAD_PAD)
        sl2 = slice(hw + hd * HEAD_PAD, hw + (hd + 1) * HEAD_PAD)
        qh = qq[:, sl]
        r = lax.rsqrt(jnp.sum(qh * qh, axis=-1, keepdims=True) * (1.0 / QK_HEAD) + EPS) * Q_SCALE
        q_ref[0, hd] = (r * (qh * qc + qq[:, sl2] * qs)).astype(bf16)
        kh = kk[:, sl] + kr
        r = lax.rsqrt(jnp.sum(kh * kh, axis=-1, keepdims=True) * (1.0 / QK_HEAD) + EPS)
        k_ref[0, hd] = (r * (kh * kc + ks_term)).astype(bf16)
        v_ref[0, hd] = (kk[:, sl2] + one_col).astype(bf16)


def _layer_spec(shape, l):
    return pl.BlockSpec((None,) + tuple(shape), lambda b, i: (l,) + (0,) * len(shape))


def _pre_call(x, ng, win, bg, qng, kvng, wq, wkv, hg, cos_t, sin_t, l, *, tm):
    bsz, seq, _ = x.shape
    nch = seq // CHUNK
    const = lambda shape: pl.BlockSpec(shape, lambda b, i: (0,) * len(shape))
    return pl.pallas_call(
        functools.partial(_pre_kernel, tm=tm),
        grid=(bsz, seq // tm),
        in_specs=[
            pl.BlockSpec((1, tm, D_MODEL), lambda b, i: (b, i, 0)),
            const((1, D_MODEL)),
            _layer_spec((D_MODEL, IN_COLS_PAD), l),
            const((1, 2 * D_MODEL)),
            const((1, Q_LORA)),
            const((1, KV_LORA)),
            _layer_spec((Q_LORA, 2 * MLA_HEADS * HEAD_PAD), l),
            _layer_spec((KV_LORA, 2 * MLA_HEADS * HEAD_PAD), l),
            _layer_spec((8, HEAD_PAD), l),
            pl.BlockSpec((tm, HEAD_PAD), lambda b, i: (i, 0)),
            pl.BlockSpec((tm, HEAD_PAD), lambda b, i: (i, 0)),
        ],
        out_specs=[
            pl.BlockSpec((tm // CHUNK, 1, N_GBLK, CHUNK, LANE), lambda b, i: (i, b, 0, 0, 0)),
            pl.BlockSpec((1, tm, 2 * D_MODEL), lambda b, i: (b, i, 0)),
            pl.BlockSpec((1, MLA_HEADS, tm, HEAD_PAD), lambda b, i: (b, 0, i, 0)),
            pl.BlockSpec((1, MLA_HEADS, tm, HEAD_PAD), lambda b, i: (b, 0, i, 0)),
            pl.BlockSpec((1, MLA_HEADS, tm, HEAD_PAD), lambda b, i: (b, 0, i, 0)),
        ],
        out_shape=[
            jax.ShapeDtypeStruct((nch, bsz, N_GBLK, CHUNK, LANE), f32),
            jax.ShapeDtypeStruct((bsz, seq, 2 * D_MODEL), bf16),
            jax.ShapeDtypeStruct((bsz, MLA_HEADS, seq, HEAD_PAD), bf16),
            jax.ShapeDtypeStruct((bsz, MLA_HEADS, seq, HEAD_PAD), bf16),
            jax.ShapeDtypeStruct((bsz, MLA_HEADS, seq, HEAD_PAD), bf16),
        ],
        compiler_params=pltpu.CompilerParams(
            dimension_semantics=("parallel", "parallel"), vmem_limit_bytes=VMEM_LIMIT),
        name="pre",
    )(x, ng, win, bg, qng, kvng, wq, wkv, hg, cos_t, sin_t)


def _attn_kernel(q_ref, k_ref, v_ref, o_ref, s_scr, m_scr, acc_scr, *, tk):
    n_heads = q_ref.shape[1]
    nk = k_ref.shape[2] // tk
    reps = tk // LANE

    def qk_chunk(hd, j):
        slot = hd % 2
        kj = k_ref[0, hd, j * tk:(j + 1) * tk, :]
        s = lax.dot_general(q_ref[0, hd], kj, (((1,), (1,)), ((), ())), preferred_element_type=f32)
        s_scr[slot, j] = s
        m = s[:, :LANE] if j == 0 else m_scr[slot]
        for c in range(1 if j == 0 else 0, reps):
            m = jnp.maximum(m, s[:, c * LANE:(c + 1) * LANE])
        m_scr[slot] = m

    def pv_chunk(hd, j):
        slot = hd % 2
        p = jnp.exp2(s_scr[slot, j] - jnp.tile(m_scr[slot], (1, reps))).astype(bf16)
        pv = _dot(p, v_ref[0, hd, j * tk:(j + 1) * tk, :])
        acc_scr[...] = pv if j == 0 else acc_scr[...] + pv

    outs = []
    for stage in range(n_heads + 1):
        if stage > 0:
            slot = (stage - 1) % 2
            m_scr[slot] = jnp.broadcast_to(jnp.max(m_scr[slot], axis=-1, keepdims=True), m_scr.shape[1:])
        for j in range(nk):
            if stage < n_heads:
                qk_chunk(stage, j)
            if stage > 0:
                pv_chunk(stage - 1, j)
        if stage > 0:
            acc = acc_scr[...]
            outs.append(acc[:, :V_HEAD] / acc[:, V_HEAD:V_HEAD + 1])
    o_ref[0] = jnp.concatenate(outs, axis=-1).astype(o_ref.dtype)


def _attn_call(q, k, v, *, tq, tk):
    bsz, nh, seq, _ = q.shape
    return pl.pallas_call(
        functools.partial(_attn_kernel, tk=tk),
        grid=(bsz, nh // 2, seq // tq),
        in_specs=[
            pl.BlockSpec((1, 2, tq, HEAD_PAD), lambda b, h, i: (b, h, i, 0)),
            pl.BlockSpec((1, 2, seq, HEAD_PAD), lambda b, h, i: (b, h, 0, 0)),
            pl.BlockSpec((1, 2, seq, HEAD_PAD), lambda b, h, i: (b, h, 0, 0)),
        ],
        out_specs=pl.BlockSpec((1, tq, 2 * V_HEAD), lambda b, h, i: (b, i, h)),
        out_shape=jax.ShapeDtypeStruct((bsz, seq, MLA_WIDTH), bf16),
        scratch_shapes=[pltpu.VMEM((2, seq // tk, tq, tk), f32), pltpu.VMEM((2, tq, LANE), f32),
                        pltpu.VMEM((tq, HEAD_PAD), f32)],
        compiler_params=pltpu.CompilerParams(
            dimension_semantics=("parallel", "parallel", "parallel"), vmem_limit_bytes=VMEM_LIMIT),
        name="attn",
    )(q, k, v)


def _chunk_rows(u_ref):
    return jnp.concatenate([u_ref[:, t, :] for t in range(CHUNK)], axis=-1).astype(bf16)


def _spread(compact_ref, spread_ref, mask_ref):
    return (_dot(compact_ref[...], spread_ref[...]) * mask_ref[...]).astype(bf16)


def _s5_state_kernel(u_ref, bc_ref, spread_ref, mask_ref, s_ref, w_scr):
    @pl.when(pl.program_id(1) == 0)
    def _():
        w_scr[...] = _spread(bc_ref, spread_ref, mask_ref)

    s_ref[...] = _dot(_chunk_rows(u_ref), w_scr[...])


def _s5_scan_kernel(s_ref, ar_ref, ai_ref, h_ref, *, nch, nb):
    half = SCAN_BLK // 2
    rows = 2 * nb
    bwd = (pl.program_id(0) // 2) % 2
    ar = jnp.broadcast_to(ar_ref[...], (rows, half))
    ai = jnp.broadcast_to(ai_ref[...], (rows, half))
    a2r, a2i = ar * ar - ai * ai, 2.0 * ar * ai
    low = lax.broadcasted_iota(jnp.int32, (rows, half), 0) < nb
    first = jnp.logical_xor(low, bwd == 1)

    def swap(a):
        return pltpu.roll(a, nb, axis=0)

    def body(i, carry):
        hr, hi = carry
        t = jnp.where(bwd == 1, nch // 2 - 1 - i, i)
        row = pl.multiple_of(t * rows, rows)
        sr = s_ref[pl.ds(row, rows), 0:half]
        si = s_ref[pl.ds(row, rows), half:SCAN_BLK]
        xr, xi = swap(sr), swap(si)
        fr, fi = jnp.where(first, sr, xr), jnp.where(first, si, xi)
        gr, gi = jnp.where(first, xr, sr), jnp.where(first, xi, si)
        mr, mi = ar * hr - ai * hi + fr, ar * hi + ai * hr + fi
        h_ref[pl.ds(row, rows), 0:half] = jnp.where(first, hr, mr)
        h_ref[pl.ds(row, rows), half:SCAN_BLK] = jnp.where(first, hi, mi)
        tr, ti = ar * fr - ai * fi + gr, ar * fi + ai * fr + gi
        return a2r * hr - a2i * hi + tr, a2r * hi + a2i * hr + ti

    zero = jnp.zeros((rows, half), f32)
    lax.fori_loop(0, nch // 2, body, (zero, zero), unroll=4)


def _s5_out_kernel(u_ref, h_ref, mc_ref, cc_ref, spread_ref, mask_m_ref, mask_c_ref, y_ref, m_scr, c_scr):
    @pl.when(pl.program_id(1) == 0)
    def _():
        m_scr[...] = _spread(mc_ref, spread_ref, mask_m_ref)
        c_scr[...] = _spread(cc_ref, spread_ref, mask_c_ref)

    y = _dot(_chunk_rows(u_ref), m_scr[...]) + _dot(h_ref[...].astype(bf16), c_scr[...])
    for t in range(CHUNK):
        y_ref[:, t, :] = y[:, t * LANE:(t + 1) * LANE]


def _s5_spread_constants():
    L, P, N, GB = CHUNK, SSM_GROUP, SSM_STATE, GROUPS_PER_BLK
    io_lane = jnp.arange(L * LANE)
    io_small = jnp.arange(L * P)
    st_lane = jnp.arange(STATE_BLK)
    st_small = jnp.arange(4 * N)
    spread_io = ((io_small[:, None] // P == io_lane[None, :] // LANE)
                 & (io_small[:, None] % P == io_lane[None, :] % P))
    spread_st = ((st_small[:, None] // (2 * N) == st_lane[None, :] // (STATE_BLK // 2))
                 & ((st_small[:, None] // N) % 2 == (st_lane[None, :] // (4 * N)) % 2)
                 & (st_small[:, None] % N == st_lane[None, :] % N))
    grp_io = (io_lane // P) % GB
    grp_st = ((st_lane // (STATE_BLK // 4)) % 2) * (GB // 2) + (st_lane // N) % (GB // 2)
    mask_m = grp_io[:, None] == grp_io[None, :]
    mask_b = grp_io[:, None] == grp_st[None, :]
    mask_c = grp_st[:, None] == grp_io[None, :]
    return tuple(a.astype(bf16) for a in (spread_io, spread_st, mask_m, mask_b, mask_c))


def _s5_call(u_perm, tabs, consts, l, *, rt):
    nch, bsz = u_perm.shape[0], u_perm.shape[1]
    kblk = CHUNK * LANE
    u2 = u_perm.reshape(nch * bsz, N_GBLK, CHUNK, LANE)
    io_spec = pl.BlockSpec((rt * bsz, None, CHUNK, LANE), lambda g, i: (i, g, 0, 0))
    mc, bc, cc, a_re, a_im = tabs
    spread_io, spread_st, mask_m, mask_b, mask_c = consts
    small_io, small_st = CHUNK * SSM_GROUP, 4 * SSM_STATE
    params = pltpu.CompilerParams(dimension_semantics=("arbitrary", "arbitrary"), vmem_limit_bytes=VMEM_LIMIT)
    whole = lambda a: pl.BlockSpec(a.shape, lambda g, i: (0,) * a.ndim)
    states = pl.pallas_call(
        _s5_state_kernel,
        grid=(N_GBLK, nch // rt),
        in_specs=[
            io_spec,
            pl.BlockSpec((None, None, kblk, small_st), lambda g, i: (l, g, 0, 0)),
            whole(spread_st), whole(mask_b),
        ],
        out_specs=pl.BlockSpec((rt * bsz, STATE_BLK), lambda g, i: (i, g)),
        out_shape=jax.ShapeDtypeStruct((nch * bsz, N_GBLK * STATE_BLK), f32),
        scratch_shapes=[pltpu.VMEM((kblk, STATE_BLK), bf16)],
        compiler_params=params,
        name="s5_state",
    )(u2, bc, spread_st, mask_b)
    n_scan = N_GBLK * STATE_BLK // SCAN_BLK
    carried = pl.pallas_call(
        functools.partial(_s5_scan_kernel, nch=nch, nb=bsz),
        grid=(n_scan,),
        in_specs=[
            pl.BlockSpec((nch * bsz, SCAN_BLK), lambda j: (0, j)),
            pl.BlockSpec((None, 1, SCAN_BLK // 2), lambda j: (l, 0, j)),
            pl.BlockSpec((None, 1, SCAN_BLK // 2), lambda j: (l, 0, j)),
        ],
        out_specs=pl.BlockSpec((nch * bsz, SCAN_BLK), lambda j: (0, j)),
        out_shape=jax.ShapeDtypeStruct((nch * bsz, N_GBLK * STATE_BLK), f32),
        compiler_params=pltpu.CompilerParams(dimension_semantics=("parallel",), vmem_limit_bytes=VMEM_LIMIT),
        name="s5_scan",
    )(states, a_re, a_im)
    y = pl.pallas_call(
        _s5_out_kernel,
        grid=(N_GBLK, nch // rt),
        in_specs=[
            io_spec,
            pl.BlockSpec((rt * bsz, STATE_BLK), lambda g, i: (i, g)),
            pl.BlockSpec((None, None, kblk, small_io), lambda g, i: (l, g, 0, 0)),
            pl.BlockSpec((None, None, STATE_BLK, small_io), lambda g, i: (l, g, 0, 0)),
            whole(spread_io), whole(mask_m), whole(mask_c),
        ],
        out_specs=io_spec,
        out_shape=jax.ShapeDtypeStruct((nch * bsz, N_GBLK, CHUNK, LANE), f32),
        scratch_shapes=[pltpu.VMEM((kblk, kblk), bf16), pltpu.VMEM((STATE_BLK, kblk), bf16)],
        compiler_params=params,
        name="s5_out",
    )(u2, carried, mc, cc, spread_io, mask_m, mask_c)
    return y.reshape(nch, bsz, N_GBLK, CHUNK, LANE)


def _s5_tables(lam_re, lam_im, log_step, b_re, b_im, c_re, c_im):
    L, G, N, P = CHUNK, SSM_GROUPS, SSM_STATE, SSM_GROUP
    hp = lax.Precision.HIGH
    step = jnp.exp(log_step)[..., None]
    zr, zi = lam_re * step, lam_im * step
    mag = jnp.exp(zr)
    abar_r, abar_i = mag * jnp.cos(zi), mag * jnp.sin(zi)
    nr, ni = abar_r - 1.0, abar_i
    den = lam_re * lam_re + lam_im * lam_im
    fr = (nr * lam_re + ni * lam_im) / den
    fi = (ni * lam_re - nr * lam_im) / den
    bbr = fr[..., None] * b_re - fi[..., None] * b_im
    bbi = fr[..., None] * b_im + fi[..., None] * b_re
    j = jnp.arange(L + 1, dtype=f32)[:, None, None, None]
    pmag = jnp.exp(j * zr)
    pr, pi = pmag * jnp.cos(j * zi), pmag * jnp.sin(j * zi)

    cpr = c_re[None, None] * pr[:L, :, :, None, :] - c_im[None, None] * pi[:L, :, :, None, :]
    cpi = c_re[None, None] * pi[:L, :, :, None, :] + c_im[None, None] * pr[:L, :, :, None, :]
    kern = (jnp.einsum('jdgpn,dgnq->jdgpq', cpr, bbr, precision=hp)
            - jnp.einsum('jdgpn,dgnq->jdgpq', cpi, bbi, precision=hp))
    s_idx = jnp.arange(L)[:, None]
    t_idx = jnp.arange(L)[None, :]
    kf = kern[:, 0][jnp.clip(t_idx - s_idx, 0, L - 1)] * (s_idx <= t_idx)[:, :, None, None, None]
    kb = kern[:, 1][jnp.clip(s_idx - t_idx, 0, L - 1)] * (s_idx >= t_idx)[:, :, None, None, None]
    mg = jnp.transpose(kf + kb, (2, 0, 4, 1, 3))
    GB = GROUPS_PER_BLK
    mc = jnp.transpose(mg.reshape(N_GBLK, GB, L, P, L, P), (0, 2, 1, 3, 4, 5)).reshape(N_GBLK, L * LANE, L * P)

    psr = jnp.stack([pr[:L, 0][::-1], pr[:L, 1]])
    psi = jnp.stack([pi[:L, 0][::-1], pi[:L, 1]])
    wr = psr[..., None] * bbr[:, None] - psi[..., None] * bbi[:, None]
    wi = psr[..., None] * bbi[:, None] + psi[..., None] * bbr[:, None]
    wst = jnp.stack([wr, wi]).reshape(2, 2, L, N_GBLK, GB, N, P)
    bc = jnp.transpose(wst, (3, 2, 4, 6, 1, 0, 5)).reshape(N_GBLK, L * LANE, 4 * N)

    pcr = jnp.stack([pr[1:, 0], pr[1:, 1][::-1]])
    pci = jnp.stack([pi[1:, 0], pi[1:, 1][::-1]])
    ccr = c_re[None, None] * pcr[:, :, :, None, :] - c_im[None, None] * pci[:, :, :, None, :]
    cci = c_re[None, None] * pci[:, :, :, None, :] + c_im[None, None] * pcr[:, :, :, None, :]
    cst = jnp.stack([ccr, -cci]).reshape(2, 2, L, N_GBLK, 2, GB // 2, P, N)
    cc = jnp.transpose(cst, (3, 1, 4, 0, 5, 7, 2, 6)).reshape(N_GBLK, STATE_BLK, L * P)

    def scan_order(a):
        a = a.reshape(2, N_GBLK, 2, GB // 2, N)
        return jnp.transpose(a, (1, 0, 2, 3, 4)).reshape(1, -1)

    return mc.astype(bf16), bc.astype(bf16), cc.astype(bf16), scan_order(pr[L]), scan_order(pi[L])


def _mix_kernel(y_ref, u_ref, o_ref, gates_ref, x_ref, d_ref, wglu_ref, bglu_ref,
                wos_ref, wom_ref, wo_ref, out_ref, *, tm):
    parts = []
    for gb in range(N_GBLK):
        sl = slice(gb * LANE, (gb + 1) * LANE)
        yg = y_ref[:, 0, gb].reshape(tm, LANE)
        ug = u_ref[:, 0, gb].reshape(tm, LANE)
        parts.append(yg + d_ref[:, sl] * ug)
    y = jax.nn.gelu(jnp.concatenate(parts, axis=-1))
    y = y * jax.nn.sigmoid(_dot(y.astype(bf16), wglu_ref[...]) + bglu_ref[...])
    a = _dot(y.astype(bf16), wos_ref[...])
    m = _dot(o_ref[0], wom_ref[...])
    gates = gates_ref[0].astype(f32)
    merged = gates[:, :D_MODEL] * a + gates[:, D_MODEL:] * m
    out_ref[0] = x_ref[0] + _dot(merged.astype(bf16), wo_ref[...])


def _mix_call(y_perm, u_perm, o, gates, x, d, wglu, bglu, wos, wom, wo, l, *, tm):
    bsz, seq, _ = x.shape
    const = lambda shape: pl.BlockSpec(shape, lambda b, i: (0,) * len(shape))
    perm_spec = pl.BlockSpec((tm // CHUNK, 1, N_GBLK, CHUNK, LANE), lambda b, i: (i, b, 0, 0, 0))
    return pl.pallas_call(
        functools.partial(_mix_kernel, tm=tm),
        grid=(bsz, seq // tm),
        in_specs=[
            perm_spec, perm_spec,
            pl.BlockSpec((1, tm, MLA_WIDTH), lambda b, i: (b, i, 0)),
            pl.BlockSpec((1, tm, 2 * D_MODEL), lambda b, i: (b, i, 0)),
            pl.BlockSpec((1, tm, D_MODEL), lambda b, i: (b, i, 0)),
            const((1, SSM_WIDTH)),
            _layer_spec((SSM_WIDTH, SSM_WIDTH), l),
            const((1, SSM_WIDTH)),
            _layer_spec((SSM_WIDTH, D_MODEL), l),
            _layer_spec((MLA_WIDTH, D_MODEL), l),
            _layer_spec((D_MODEL, D_MODEL), l),
        ],
        out_specs=pl.BlockSpec((1, tm, D_MODEL), lambda b, i: (b, i, 0)),
        out_shape=jax.ShapeDtypeStruct(x.shape, f32),
        compiler_params=pltpu.CompilerParams(
            dimension_semantics=("parallel", "parallel"), vmem_limit_bytes=VMEM_LIMIT),
        name="mix",
    )(y_perm, u_perm, o, gates, x, d, wglu, bglu, wos, wom, wo)


def _ffn_kernel(x_ref, g_ref, w1_ref, w2_ref, out_ref, *, fb):
    x = x_ref[...]
    h = _rms(x, g_ref[...]).astype(bf16)
    acc = x
    for j in range(D_FF // fb):
        a = jnp.maximum(_dot(h, w1_ref[:, j * fb:(j + 1) * fb]), 0.0)
        acc = acc + _dot((a * a).astype(bf16), w2_ref[j * fb:(j + 1) * fb, :])
    out_ref[...] = acc


def _ffn_call(x2, g, w1, w2, l, *, tm, fb):
    ntok = x2.shape[0]
    return pl.pallas_call(
        functools.partial(_ffn_kernel, fb=fb),
        grid=(ntok // tm,),
        in_specs=[
            pl.BlockSpec((tm, D_MODEL), lambda i: (i, 0)),
            pl.BlockSpec((1, D_MODEL), lambda i: (0, 0)),
            pl.BlockSpec((None, D_MODEL, D_FF), lambda i: (l, 0, 0)),
            pl.BlockSpec((None, D_FF, D_MODEL), lambda i: (l, 0, 0)),
        ],
        out_specs=pl.BlockSpec((tm, D_MODEL), lambda i: (i, 0)),
        out_shape=jax.ShapeDtypeStruct(x2.shape, f32),
        compiler_params=pltpu.CompilerParams(dimension_semantics=("parallel",), vmem_limit_bytes=VMEM_LIMIT),
        name="ffn",
    )(x2, g, w1, w2)


def _rope_swap(a):
    half = QK_ROPE // 2
    return jnp.concatenate([a[..., half:], a[..., :half]], axis=-1)


def _place_rope(a):
    pad = [(0, 0)] * (a.ndim - 1) + [(QK_NOPE, HEAD_PAD - QK_HEAD)]
    return jnp.pad(a, pad)


def _pad_head(a, width):
    pad = [(0, 0)] * (a.ndim - 1) + [(0, HEAD_PAD - width)]
    a = jnp.pad(a, pad)
    return a.reshape(a.shape[:-2] + (MLA_HEADS * HEAD_PAD,))


def _layer_params(w, w_q, w_kv, qg, kg):
    o3 = SSM_WIDTH + Q_LORA + KV_LORA
    o4 = o3 + QK_ROPE
    wkr = w[:, o3:o4]
    win = jnp.concatenate([w[:, :o3], w[:, o4:], _place_rope(wkr), _place_rope(_rope_swap(wkr))], axis=1).astype(bf16)
    wq = w_q.reshape(Q_LORA, MLA_HEADS, QK_HEAD)
    wq_sw = _place_rope(_rope_swap(wq[..., QK_NOPE:])).reshape(Q_LORA, MLA_HEADS * HEAD_PAD)
    wq = jnp.concatenate([_pad_head(wq, QK_HEAD), wq_sw], axis=1).astype(bf16)
    wkv = w_kv.reshape(KV_LORA, MLA_HEADS, QK_NOPE + V_HEAD)
    wkv = jnp.concatenate([_pad_head(wkv[..., :QK_NOPE], QK_NOPE), _pad_head(wkv[..., QK_NOPE:], V_HEAD)],
                          axis=1).astype(bf16)
    one_col = jnp.zeros((HEAD_PAD,), f32).at[V_HEAD].set(1.0)
    zero = jnp.zeros((HEAD_PAD,), f32)
    hg = jnp.stack([
        jnp.pad(qg, (0, HEAD_PAD - QK_HEAD)), _place_rope(_rope_swap(qg[QK_NOPE:])),
        jnp.pad(kg, (0, HEAD_PAD - QK_HEAD)), _place_rope(_rope_swap(kg[QK_NOPE:])),
        one_col, zero, zero, zero])
    return win, wq, wkv, hg


def _rope_tables(seq):
    half = QK_ROPE // 2
    inv_freq = ROPE_THETA ** (-jnp.arange(half, dtype=f32) / half)
    ang = jnp.arange(seq, dtype=f32)[:, None] * inv_freq[None, :]
    cos, sin = jnp.cos(ang), jnp.sin(ang)
    ones = jnp.ones((seq, QK_NOPE), f32)
    zpad = jnp.zeros((seq, HEAD_PAD - QK_HEAD), f32)
    cos_t = jnp.concatenate([ones, cos, cos, zpad], axis=1)
    sin_t = jnp.concatenate([0.0 * ones, -sin, sin, zpad], axis=1)
    return cos_t, sin_t


def kernel(x, mix_norm_g, w_in, b_gate, ssm_lam_re, ssm_lam_im, ssm_log_step, ssm_b_re, ssm_b_im, ssm_c_re, ssm_c_im, ssm_d, w_glu, b_glu, w_out_ssm, q_norm_g, kv_norm_g, w_q_up, w_kv_up, q_head_g, k_head_g, w_out_mla, w_o, ffn_norm_g, w_ff1, w_ff2):
    bsz, seq, _ = x.shape
    depth = w_in.shape[0]
    nch = seq // CHUNK
    tm = min(512, seq)
    tq = min(512, seq)
    tk = min(4096, seq)
    rt = min(256, nch)
    assert 2 * bsz == 8, "the S5 chunk scan packs two chunks of `bsz` batch rows into one 8-sublane tile"
    cos_t, sin_t = _rope_tables(seq)
    win, wq, wkv, hg = jax.vmap(_layer_params)(w_in, w_q_up, w_kv_up, q_head_g, k_head_g)
    tabs = jax.vmap(_s5_tables)(ssm_lam_re, ssm_lam_im, ssm_log_step, ssm_b_re, ssm_b_im, ssm_c_re, ssm_c_im)
    consts = _s5_spread_constants()
    wglu, wos, wom, wo, w1, w2 = (a.astype(bf16) for a in (w_glu, w_out_ssm, w_out_mla, w_o, w_ff1, w_ff2))
    for l in range(depth):
        u_perm, gates, q, k, v = _pre_call(
            x, mix_norm_g[l][None], win, b_gate[l].reshape(1, 2 * D_MODEL), q_norm_g[l][None],
            kv_norm_g[l][None], wq, wkv, hg, cos_t, sin_t, l, tm=tm)
        o = _attn_call(q, k, v, tq=tq, tk=tk)
        y_perm = _s5_call(u_perm, tabs, consts, l, rt=rt)
        x = _mix_call(y_perm, u_perm, o, gates, x, ssm_d[l].reshape(1, SSM_WIDTH), wglu, b_glu[l][None],
                      wos, wom, wo, l, tm=tm)
        x = _ffn_call(x.reshape(bsz * seq, D_MODEL), ffn_norm_g[l][None], w1, w2, l,
                      tm=tm, fb=1024).reshape(bsz, seq, D_MODEL)
    return x
```

```python
import functools
import math

import jax
import jax.numpy as jnp
from jax import lax
from jax.experimental import pallas as pl
from jax.experimental.pallas import tpu as pltpu

D_MODEL = 1024
SSM_WIDTH = 512
SSM_GROUP = 16
SSM_GROUPS = 32
SSM_STATE = 64
MLA_HEADS = 8
QK_NOPE = 64
QK_ROPE = 32
QK_HEAD = QK_NOPE + QK_ROPE
V_HEAD = 64
Q_LORA = 384
KV_LORA = 256
MLA_WIDTH = MLA_HEADS * V_HEAD
ROPE_THETA = 10000.0
D_FF = 4 * D_MODEL
EPS = 1e-6

LANE = 128
HEAD_PAD = LANE
CHUNK = 8
GROUPS_PER_BLK = LANE // SSM_GROUP
N_GBLK = SSM_GROUPS // GROUPS_PER_BLK
STATE_BLK = 2 * 2 * GROUPS_PER_BLK * SSM_STATE
SCAN_BLK = STATE_BLK // 4
VMEM_LIMIT = 56 * 1024 * 1024
Q_SCALE = QK_HEAD ** -0.5 * math.log2(math.e)

_C_U = 0
_C_CQ = _C_U + SSM_WIDTH
_C_CKV = _C_CQ + Q_LORA
_C_GATE = _C_CKV + KV_LORA
_C_KR = _C_GATE + 2 * D_MODEL
_C_KRS = _C_KR + HEAD_PAD
IN_COLS_PAD = _C_KRS + HEAD_PAD

bf16 = jnp.bfloat16
f32 = jnp.float32


def _dot(a, b):
    return jnp.dot(a, b, preferred_element_type=f32)


def _rms(x, g):
    return x * lax.rsqrt(jnp.mean(x * x, axis=-1, keepdims=True) + EPS) * g


def _pre_kernel(x_ref, ng_ref, win_ref, bg_ref, qng_ref, kvng_ref, wq_ref, wkv_ref,
                hg_ref, cos_ref, sin_ref,
                u_ref, gates_ref, q_ref, k_ref, v_ref, *, tm):
    x = x_ref[0]
    h = _rms(x, ng_ref[...]).astype(bf16)
    proj = _dot(h, win_ref[...])
    u = proj[:, _C_U:_C_CQ]
    for gb in range(N_GBLK):
        u_ref[:, 0, gb] = u[:, gb * LANE:(gb + 1) * LANE].reshape(tm // CHUNK, CHUNK, LANE)
    gates_ref[0] = jax.nn.sigmoid(proj[:, _C_GATE:_C_KR] + bg_ref[...]).astype(bf16)

    cq = _rms(proj[:, _C_CQ:_C_CKV], qng_ref[...]).astype(bf16)
    ckv = _rms(proj[:, _C_CKV:_C_GATE], kvng_ref[...]).astype(bf16)
    qq = _dot(cq, wq_ref[...])
    kk = _dot(ckv, wkv_ref[...])
    kr = proj[:, _C_KR:_C_KRS]
    krs = proj[:, _C_KRS:IN_COLS_PAD]
    cos = cos_ref[...]
    sin = sin_ref[...]
    hg = hg_ref[...]
    qc, qs = hg[0:1] * cos, hg[1:2] * sin
    kc, ks = hg[2:3] * cos, hg[3:4] * sin
    ks_term = krs * ks
    one_col = hg[4:5]
    hw = MLA_HEADS * HEAD_PAD
    for hd in range(MLA_HEADS):
        sl = slice(hd * HEAD_PAD, (hd + 1) * HEAD_PAD)
        sl2 = slice(hw + hd * HEAD_PAD, hw + (hd + 1) * HEAD_PAD)
        qh = qq[:, sl]
        r = lax.rsqrt(jnp.sum(qh * qh, axis=-1, keepdims=True) * (1.0 / QK_HEAD) + EPS) * Q_SCALE
        q_ref[0, hd] = (r * (qh * qc + qq[:, sl2] * qs)).astype(bf16)
        kh = kk[:, sl] + kr
        r = lax.rsqrt(jnp.sum(kh * kh, axis=-1, keepdims=True) * (1.0 / QK_HEAD) + EPS)
        k_ref[0, hd] = (r * (kh * kc + ks_term)).astype(bf16)
        v_ref[0, hd] = (kk[:, sl2] + one_col).astype(bf16)


def _layer_spec(shape, l):
    return pl.BlockSpec((None,) + tuple(shape), lambda b, i: (l,) + (0,) * len(shape))


def _pre_call(x, ng, win, bg, qng, kvng, wq, wkv, hg, cos_t, sin_t, l, *, tm):
    bsz, seq, _ = x.shape
    nch = seq // CHUNK
    const = lambda shape: pl.BlockSpec(shape, lambda b, i: (0,) * len(shape))
    return pl.pallas_call(
        functools.partial(_pre_kernel, tm=tm),
        grid=(bsz, seq // tm),
        in_specs=[
            pl.BlockSpec((1, tm, D_MODEL), lambda b, i: (b, i, 0)),
            const((1, D_MODEL)),
            _layer_spec((D_MODEL, IN_COLS_PAD), l),
            const((1, 2 * D_MODEL)),
            const((1, Q_LORA)),
            const((1, KV_LORA)),
            _layer_spec((Q_LORA, 2 * MLA_HEADS * HEAD_PAD), l),
            _layer_spec((KV_LORA, 2 * MLA_HEADS * HEAD_PAD), l),
            _layer_spec((8, HEAD_PAD), l),
            pl.BlockSpec((tm, HEAD_PAD), lambda b, i: (i, 0)),
            pl.BlockSpec((tm, HEAD_PAD), lambda b, i: (i, 0)),
        ],
        out_specs=[
            pl.BlockSpec((tm // CHUNK, 1, N_GBLK, CHUNK, LANE), lambda b, i: (i, b, 0, 0, 0)),
            pl.BlockSpec((1, tm, 2 * D_MODEL), lambda b, i: (b, i, 0)),
            pl.BlockSpec((1, MLA_HEADS, tm, HEAD_PAD), lambda b, i: (b, 0, i, 0)),
            pl.BlockSpec((1, MLA_HEADS, tm, HEAD_PAD), lambda b, i: (b, 0, i, 0)),
            pl.BlockSpec((1, MLA_HEADS, tm, HEAD_PAD), lambda b, i: (b, 0, i, 0)),
        ],
        out_shape=[
            jax.ShapeDtypeStruct((nch, bsz, N_GBLK, CHUNK, LANE), f32),
            jax.ShapeDtypeStruct((bsz, seq, 2 * D_MODEL), bf16),
            jax.ShapeDtypeStruct((bsz, MLA_HEADS, seq, HEAD_PAD), bf16),
            jax.ShapeDtypeStruct((bsz, MLA_HEADS, seq, HEAD_PAD), bf16),
            jax.ShapeDtypeStruct((bsz, MLA_HEADS, seq, HEAD_PAD), bf16),
        ],
        compiler_params=pltpu.CompilerParams(
            dimension_semantics=("parallel", "parallel"), vmem_limit_bytes=VMEM_LIMIT),
        name="pre",
    )(x, ng, win, bg, qng, kvng, wq, wkv, hg, cos_t, sin_t)


def _attn_kernel(q_ref, k_ref, v_ref, o_ref, s_scr, m_scr, acc_scr, *, tk):
    n_heads = q_ref.shape[1]
    nk = k_ref.shape[2] // tk
    reps = tk // LANE

    def qk_chunk(hd, j):
        slot = hd % 2
        kj = k_ref[0, hd, j * tk:(j + 1) * tk, :]
        s = lax.dot_general(q_ref[0, hd], kj, (((1,), (1,)), ((), ())), preferred_element_type=f32)
        s_scr[slot, j] = s
        m = s[:, :LANE] if j == 0 else m_scr[slot]
        for c in range(1 if j == 0 else 0, reps):
            m = jnp.maximum(m, s[:, c * LANE:(c + 1) * LANE])
        m_scr[slot] = m

    def pv_chunk(hd, j):
        slot = hd % 2
        p = jnp.exp2(s_scr[slot, j] - jnp.tile(m_scr[slot], (1, reps))).astype(bf16)
        pv = _dot(p, v_ref[0, hd, j * tk:(j + 1) * tk, :])
        acc_scr[...] = pv if j == 0 else acc_scr[...] + pv

    outs = []
    for stage in range(n_heads + 1):
        if stage > 0:
            slot = (stage - 1) % 2
            m_scr[slot] = jnp.broadcast_to(jnp.max(m_scr[slot], axis=-1, keepdims=True), m_scr.shape[1:])
        for j in range(nk):
            if stage < n_heads:
                qk_chunk(stage, j)
            if stage > 0:
                pv_chunk(stage - 1, j)
        if stage > 0:
            acc = acc_scr[...]
            outs.append(acc[:, :V_HEAD] / acc[:, V_HEAD:V_HEAD + 1])
    o_ref[0] = jnp.concatenate(outs, axis=-1).astype(o_ref.dtype)


def _attn_call(q, k, v, *, tq, tk):
    bsz, nh, seq, _ = q.shape
    return pl.pallas_call(
        functools.partial(_attn_kernel, tk=tk),
        grid=(bsz, nh // 2, seq // tq),
        in_specs=[
            pl.BlockSpec((1, 2, tq, HEAD_PAD), lambda b, h, i: (b, h, i, 0)),
            pl.BlockSpec((1, 2, seq, HEAD_PAD), lambda b, h, i: (b, h, 0, 0)),
            pl.BlockSpec((1, 2, seq, HEAD_PAD), lambda b, h, i: (b, h, 0, 0)),
        ],
        out_specs=pl.BlockSpec((1, tq, 2 * V_HEAD), lambda b, h, i: (b, i, h)),
        out_shape=jax.ShapeDtypeStruct((bsz, seq, MLA_WIDTH), bf16),
        scratch_shapes=[pltpu.VMEM((2, seq // tk, tq, tk), f32), pltpu.VMEM((2, tq, LANE), f32),
                        pltpu.VMEM((tq, HEAD_PAD), f32)],
        compiler_params=pltpu.CompilerParams(
            dimension_semantics=("parallel", "parallel", "parallel"), vmem_limit_bytes=VMEM_LIMIT),
        name="attn",
    )(q, k, v)


def _chunk_rows(u_ref):
    return jnp.concatenate([u_ref[:, t, :] for t in range(CHUNK)], axis=-1).astype(bf16)


def _spread(compact_ref, spread_ref, mask_ref):
    return (_dot(compact_ref[...], spread_ref[...]) * mask_ref[...]).astype(bf16)


def _s5_state_kernel(u_ref, bc_ref, spread_ref, mask_ref, s_ref, w_scr):
    @pl.when(pl.program_id(1) == 0)
    def _():
        w_scr[...] = _spread(bc_ref, spread_ref, mask_ref)

    s_ref[...] = _dot(_chunk_rows(u_ref), w_scr[...])


def _s5_scan_kernel(s_ref, ar_ref, ai_ref, h_ref, *, nch, nb):
    half = SCAN_BLK // 2
    rows = 2 * nb
    bwd = (pl.program_id(0) // 2) % 2
    ar = jnp.broadcast_to(ar_ref[...], (rows, half))
    ai = jnp.broadcast_to(ai_ref[...], (rows, half))
    a2r, a2i = ar * ar - ai * ai, 2.0 * ar * ai
    low = lax.broadcasted_iota(jnp.int32, (rows, half), 0) < nb
    first = jnp.logical_xor(low, bwd == 1)

    def swap(a):
        return pltpu.roll(a, nb, axis=0)

    def body(i, carry):
        hr, hi = carry
        t = jnp.where(bwd == 1, nch // 2 - 1 - i, i)
        row = pl.multiple_of(t * rows, rows)
        sr = s_ref[pl.ds(row, rows), 0:half]
        si = s_ref[pl.ds(row, rows), half:SCAN_BLK]
        xr, xi = swap(sr), swap(si)
        fr, fi = jnp.where(first, sr, xr), jnp.where(first, si, xi)
        gr, gi = jnp.where(first, xr, sr), jnp.where(first, xi, si)
        mr, mi = ar * hr - ai * hi + fr, ar * hi + ai * hr + fi
        h_ref[pl.ds(row, rows), 0:half] = jnp.where(first, hr, mr)
        h_ref[pl.ds(row, rows), half:SCAN_BLK] = jnp.where(first, hi, mi)
        tr, ti = ar * fr - ai * fi + gr, ar * fi + ai * fr + gi
        return a2r * hr - a2i * hi + tr, a2r * hi + a2i * hr + ti

    zero = jnp.zeros((rows, half), f32)
    lax.fori_loop(0, nch // 2, body, (zero, zero), unroll=4)


def _s5_out_kernel(u_ref, h_ref, mc_ref, cc_ref, spread_ref, mask_m_ref, mask_c_ref, y_ref, m_scr, c_scr):
    @pl.when(pl.program_id(1) == 0)
    def _():
        m_scr[...] = _spread(mc_ref, spread_ref, mask_m_ref)
        c_scr[...] = _spread(cc_ref, spread_ref, mask_c_ref)

    y = _dot(_chunk_rows(u_ref), m_scr[...]) + _dot(h_ref[...].astype(bf16), c_scr[...])
    for t in range(CHUNK):
        y_ref[:, t, :] = y[:, t * LANE:(t + 1) * LANE]


def _s5_spread_constants():
    L, P, N, GB = CHUNK, SSM_GROUP, SSM_STATE, GROUPS_PER_BLK
    io_lane = jnp.arange(L * LANE)
    io_small = jnp.arange(L * P)
    st_lane = jnp.arange(STATE_BLK)
    st_small = jnp.arange(4 * N)
    spread_io = ((io_small[:, None] // P == io_lane[None, :] // LANE)
                 & (io_small[:, None] % P == io_lane[None, :] % P))
    spread_st = ((st_small[:, None] // (2 * N) == st_lane[None, :] // (STATE_BLK // 2))
                 & ((st_small[:, None] // N) % 2 == (st_lane[None, :] // (4 * N)) % 2)
                 & (st_small[:, None] % N == st_lane[None, :] % N))
    grp_io = (io_lane // P) % GB
    grp_st = ((st_lane // (STATE_BLK // 4)) % 2) * (GB // 2) + (st_lane // N) % (GB // 2)
    mask_m = grp_io[:, None] == grp_io[None, :]
    mask_b = grp_io[:, None] == grp_st[None, :]
    mask_c = grp_st[:, None] == grp_io[None, :]
    return tuple(a.astype(bf16) for a in (spread_io, spread_st, mask_m, mask_b, mask_c))


def _s5_call(u_perm, tabs, consts, l, *, rt):
    nch, bsz = u_perm.shape[0], u_perm.shape[1]
    kblk = CHUNK * LANE
    u2 = u_perm.reshape(nch * bsz, N_GBLK, CHUNK, LANE)
    io_spec = pl.BlockSpec((rt * bsz, None, CHUNK, LANE), lambda g, i: (i, g, 0, 0))
    mc, bc, cc, a_re, a_im = tabs
    spread_io, spread_st, mask_m, mask_b, mask_c = consts
    small_io, small_st = CHUNK * SSM_GROUP, 4 * SSM_STATE
    params = pltpu.CompilerParams(dimension_semantics=("arbitrary", "arbitrary"), vmem_limit_bytes=VMEM_LIMIT)
    whole = lambda a: pl.BlockSpec(a.shape, lambda g, i: (0,) * a.ndim)
    states = pl.pallas_call(
        _s5_state_kernel,
        grid=(N_GBLK, nch // rt),
        in_specs=[
            io_spec,
            pl.BlockSpec((None, None, kblk, small_st), lambda g, i: (l, g, 0, 0)),
            whole(spread_st), whole(mask_b),
        ],
        out_specs=pl.BlockSpec((rt * bsz, STATE_BLK), lambda g, i: (i, g)),
        out_shape=jax.ShapeDtypeStruct((nch * bsz, N_GBLK * STATE_BLK), f32),
        scratch_shapes=[pltpu.VMEM((kblk, STATE_BLK), bf16)],
        compiler_params=params,
        name="s5_state",
    )(u2, bc, spread_st, mask_b)
    n_scan = N_GBLK * STATE_BLK // SCAN_BLK
    carried = pl.pallas_call(
        functools.partial(_s5_scan_kernel, nch=nch, nb=bsz),
        grid=(n_scan,),
        in_specs=[
            pl.BlockSpec((nch * bsz, SCAN_BLK), lambda j: (0, j)),
            pl.BlockSpec((None, 1, SCAN_BLK // 2), lambda j: (l, 0, j)),
            pl.BlockSpec((None, 1, SCAN_BLK // 2), lambda j: (l, 0, j)),
        ],
        out_specs=pl.BlockSpec((nch * bsz, SCAN_BLK), lambda j: (0, j)),
        out_shape=jax.ShapeDtypeStruct((nch * bsz, N_GBLK * STATE_BLK), f32),
        compiler_params=pltpu.CompilerParams(dimension_semantics=("parallel",), vmem_limit_bytes=VMEM_LIMIT),
        name="s5_scan",
    )(states, a_re, a_im)
    y = pl.pallas_call(
        _s5_out_kernel,
        grid=(N_GBLK, nch // rt),
        in_specs=[
            io_spec,
            pl.BlockSpec((rt * bsz, STATE_BLK), lambda g, i: (i, g)),
            pl.BlockSpec((None, None, kblk, small_io), lambda g, i: (l, g, 0, 0)),
            pl.BlockSpec((None, None, STATE_BLK, small_io), lambda g, i: (l, g, 0, 0)),
            whole(spread_io), whole(mask_m), whole(mask_c),
        ],
        out_specs=io_spec,
        out_shape=jax.ShapeDtypeStruct((nch * bsz, N_GBLK, CHUNK, LANE), f32),
        scratch_shapes=[pltpu.VMEM((kblk, kblk), bf16), pltpu.VMEM((STATE_BLK, kblk), bf16)],
        compiler_params=params,
        name="s5_out",
    )(u2, carried, mc, cc, spread_io, mask_m, mask_c)
    return y.reshape(nch, bsz, N_GBLK, CHUNK, LANE)


def _s5_tables(lam_re, lam_im, log_step, b_re, b_im, c_re, c_im):
    L, G, N, P = CHUNK, SSM_GROUPS, SSM_STATE, SSM_GROUP
    hp = lax.Precision.HIGH
    step = jnp.exp(log_step)[..., None]
    zr, zi = lam_re * step, lam_im * step
    mag = jnp.exp(zr)
    abar_r, abar_i = mag * jnp.cos(zi), mag * jnp.sin(zi)
    nr, ni = abar_r - 1.0, abar_i
    den = lam_re * lam_re + lam_im * lam_im
    fr = (nr * lam_re + ni * lam_im) / den
    fi = (ni * lam_re - nr * lam_im) / den
    bbr = fr[..., None] * b_re - fi[..., None] * b_im
    bbi = fr[..., None] * b_im + fi[..., None] * b_re
    j = jnp.arange(L + 1, dtype=f32)[:, None, None, None]
    pmag = jnp.exp(j * zr)
    pr, pi = pmag * jnp.cos(j * zi), pmag * jnp.sin(j * zi)

    cpr = c_re[None, None] * pr[:L, :, :, None, :] - c_im[None, None] * pi[:L, :, :, None, :]
    cpi = c_re[None, None] * pi[:L, :, :, None, :] + c_im[None, None] * pr[:L, :, :, None, :]
    kern = (jnp.einsum('jdgpn,dgnq->jdgpq', cpr, bbr, precision=hp)
            - jnp.einsum('jdgpn,dgnq->jdgpq', cpi, bbi, precision=hp))
    s_idx = jnp.arange(L)[:, None]
    t_idx = jnp.arange(L)[None, :]
    kf = kern[:, 0][jnp.clip(t_idx - s_idx, 0, L - 1)] * (s_idx <= t_idx)[:, :, None, None, None]
    kb = kern[:, 1][jnp.clip(s_idx - t_idx, 0, L - 1)] * (s_idx >= t_idx)[:, :, None, None, None]
    mg = jnp.transpose(kf + kb, (2, 0, 4, 1, 3))
    GB = GROUPS_PER_BLK
    mc = jnp.transpose(mg.reshape(N_GBLK, GB, L, P, L, P), (0, 2, 1, 3, 4, 5)).reshape(N_GBLK, L * LANE, L * P)

    psr = jnp.stack([pr[:L, 0][::-1], pr[:L, 1]])
    psi = jnp.stack([pi[:L, 0][::-1], pi[:L, 1]])
    wr = psr[..., None] * bbr[:, None] - psi[..., None] * bbi[:, None]
    wi = psr[..., None] * bbi[:, None] + psi[..., None] * bbr[:, None]
    wst = jnp.stack([wr, wi]).reshape(2, 2, L, N_GBLK, GB, N, P)
    bc = jnp.transpose(wst, (3, 2, 4, 6, 1, 0, 5)).reshape(N_GBLK, L * LANE, 4 * N)

    pcr = jnp.stack([pr[1:, 0], pr[1:, 1][::-1]])
    pci = jnp.stack([pi[1:, 0], pi[1:, 1][::-1]])
    ccr = c_re[None, None] * pcr[:, :, :, None, :] - c_im[None, None] * pci[:, :, :, None, :]
    cci = c_re[None, None] * pci[:, :, :, None, :] + c_im[None, None] * pcr[:, :, :, None, :]
    cst = jnp.stack([ccr, -cci]).reshape(2, 2, L, N_GBLK, 2, GB // 2, P, N)
    cc = jnp.transpose(cst, (3, 1, 4, 0, 5, 7, 2, 6)).reshape(N_GBLK, STATE_BLK, L * P)

    def scan_order(a):
        a = a.reshape(2, N_GBLK, 2, GB // 2, N)
        return jnp.transpose(a, (1, 0, 2, 3, 4)).reshape(1, -1)

    return mc.astype(bf16), bc.astype(bf16), cc.astype(bf16), scan_order(pr[L]), scan_order(pi[L])


def _post_kernel(y_ref, u_ref, o_ref, gates_ref, x_ref, d_ref, wglu_ref, bglu_ref,
                 wos_ref, wom_ref, wo_ref, fg_ref, w1_ref, w2_ref, out_ref, *, tm, fb):
    parts = []
    for gb in range(N_GBLK):
        sl = slice(gb * LANE, (gb + 1) * LANE)
        yg = y_ref[:, 0, gb].reshape(tm, LANE)
        ug = u_ref[:, 0, gb].reshape(tm, LANE)
        parts.append(yg + d_ref[:, sl] * ug)
    y = jax.nn.gelu(jnp.concatenate(parts, axis=-1))
    y = y * jax.nn.sigmoid(_dot(y.astype(bf16), wglu_ref[...]) + bglu_ref[...])
    a = _dot(y.astype(bf16), wos_ref[...])
    m = _dot(o_ref[0], wom_ref[...])
    gates = gates_ref[0].astype(f32)
    merged = gates[:, :D_MODEL] * a + gates[:, D_MODEL:] * m
    x = x_ref[0] + _dot(merged.astype(bf16), wo_ref[...])
    h = _rms(x, fg_ref[...]).astype(bf16)
    acc = x
    for j in range(D_FF // fb):
        a = jnp.maximum(_dot(h, w1_ref[:, j * fb:(j + 1) * fb]), 0.0)
        acc = acc + _dot((a * a).astype(bf16), w2_ref[j * fb:(j + 1) * fb, :])
    out_ref[0] = acc


def _post_call(y_perm, u_perm, o, gates, x, d, wglu, bglu, wos, wom, wo, fg, w1, w2, l, *, tm, fb):
    bsz, seq, _ = x.shape
    const = lambda shape: pl.BlockSpec(shape, lambda b, i: (0,) * len(shape))
    resident = lambda shape: pl.BlockSpec((None,) + shape, lambda b, i: (l,) + (0,) * len(shape),
                                          pipeline_mode=pl.Buffered(1))
    perm_spec = pl.BlockSpec((tm // CHUNK, 1, N_GBLK, CHUNK, LANE), lambda b, i: (i, b, 0, 0, 0))
    return pl.pallas_call(
        functools.partial(_post_kernel, tm=tm, fb=fb),
        grid=(bsz, seq // tm),
        in_specs=[
            perm_spec, perm_spec,
            pl.BlockSpec((1, tm, MLA_WIDTH), lambda b, i: (b, i, 0)),
            pl.BlockSpec((1, tm, 2 * D_MODEL), lambda b, i: (b, i, 0)),
            pl.BlockSpec((1, tm, D_MODEL), lambda b, i: (b, i, 0)),
            const((1, SSM_WIDTH)),
            resident((SSM_WIDTH, SSM_WIDTH)),
            const((1, SSM_WIDTH)),
            resident((SSM_WIDTH, D_MODEL)),
            resident((MLA_WIDTH, D_MODEL)),
            resident((D_MODEL, D_MODEL)),
            const((1, D_MODEL)),
            resident((D_MODEL, D_FF)),
            resident((D_FF, D_MODEL)),
        ],
        out_specs=pl.BlockSpec((1, tm, D_MODEL), lambda b, i: (b, i, 0)),
        out_shape=jax.ShapeDtypeStruct(x.shape, f32),
        compiler_params=pltpu.CompilerParams(
            dimension_semantics=("parallel", "parallel"), vmem_limit_bytes=VMEM_LIMIT),
        name="post",
    )(y_perm, u_perm, o, gates, x, d, wglu, bglu, wos, wom, wo, fg, w1, w2)


def _rope_swap(a):
    half = QK_ROPE // 2
    return jnp.concatenate([a[..., half:], a[..., :half]], axis=-1)


def _place_rope(a):
    pad = [(0, 0)] * (a.ndim - 1) + [(QK_NOPE, HEAD_PAD - QK_HEAD)]
    return jnp.pad(a, pad)


def _pad_head(a, width):
    pad = [(0, 0)] * (a.ndim - 1) + [(0, HEAD_PAD - width)]
    a = jnp.pad(a, pad)
    return a.reshape(a.shape[:-2] + (MLA_HEADS * HEAD_PAD,))


def _layer_params(w, w_q, w_kv, qg, kg):
    o3 = SSM_WIDTH + Q_LORA + KV_LORA
    o4 = o3 + QK_ROPE
    wkr = w[:, o3:o4]
    win = jnp.concatenate([w[:, :o3], w[:, o4:], _place_rope(wkr), _place_rope(_rope_swap(wkr))], axis=1).astype(bf16)
    wq = w_q.reshape(Q_LORA, MLA_HEADS, QK_HEAD)
    wq_sw = _place_rope(_rope_swap(wq[..., QK_NOPE:])).reshape(Q_LORA, MLA_HEADS * HEAD_PAD)
    wq = jnp.concatenate([_pad_head(wq, QK_HEAD), wq_sw], axis=1).astype(bf16)
    wkv = w_kv.reshape(KV_LORA, MLA_HEADS, QK_NOPE + V_HEAD)
    wkv = jnp.concatenate([_pad_head(wkv[..., :QK_NOPE], QK_NOPE), _pad_head(wkv[..., QK_NOPE:], V_HEAD)],
                          axis=1).astype(bf16)
    one_col = jnp.zeros((HEAD_PAD,), f32).at[V_HEAD].set(1.0)
    zero = jnp.zeros((HEAD_PAD,), f32)
    hg = jnp.stack([
        jnp.pad(qg, (0, HEAD_PAD - QK_HEAD)), _place_rope(_rope_swap(qg[QK_NOPE:])),
        jnp.pad(kg, (0, HEAD_PAD - QK_HEAD)), _place_rope(_rope_swap(kg[QK_NOPE:])),
        one_col, zero, zero, zero])
    return win, wq, wkv, hg


def _rope_tables(seq):
    half = QK_ROPE // 2
    inv_freq = ROPE_THETA ** (-jnp.arange(half, dtype=f32) / half)
    ang = jnp.arange(seq, dtype=f32)[:, None] * inv_freq[None, :]
    cos, sin = jnp.cos(ang), jnp.sin(ang)
    ones = jnp.ones((seq, QK_NOPE), f32)
    zpad = jnp.zeros((seq, HEAD_PAD - QK_HEAD), f32)
    cos_t = jnp.concatenate([ones, cos, cos, zpad], axis=1)
    sin_t = jnp.concatenate([0.0 * ones, -sin, sin, zpad], axis=1)
    return cos_t, sin_t


def kernel(x, mix_norm_g, w_in, b_gate, ssm_lam_re, ssm_lam_im, ssm_log_step, ssm_b_re, ssm_b_im, ssm_c_re, ssm_c_im, ssm_d, w_glu, b_glu, w_out_ssm, q_norm_g, kv_norm_g, w_q_up, w_kv_up, q_head_g, k_head_g, w_out_mla, w_o, ffn_norm_g, w_ff1, w_ff2):
    bsz, seq, _ = x.shape
    depth = w_in.shape[0]
    nch = seq // CHUNK
    tm = min(512, seq)
    tm_post = min(256, seq)
    tq = min(512, seq)
    tk = min(4096, seq)
    rt = min(256, nch)
    assert 2 * bsz == 8, "the S5 chunk scan packs two chunks of `bsz` batch rows into one 8-sublane tile"
    cos_t, sin_t = _rope_tables(seq)
    win, wq, wkv, hg = jax.vmap(_layer_params)(w_in, w_q_up, w_kv_up, q_head_g, k_head_g)
    tabs = jax.vmap(_s5_tables)(ssm_lam_re, ssm_lam_im, ssm_log_step, ssm_b_re, ssm_b_im, ssm_c_re, ssm_c_im)
    consts = _s5_spread_constants()
    wglu, wos, wom, wo, w1, w2 = (a.astype(bf16) for a in (w_glu, w_out_ssm, w_out_mla, w_o, w_ff1, w_ff2))
    for l in range(depth):
        u_perm, gates, q, k, v = _pre_call(
            x, mix_norm_g[l][None], win, b_gate[l].reshape(1, 2 * D_MODEL), q_norm_g[l][None],
            kv_norm_g[l][None], wq, wkv, hg, cos_t, sin_t, l, tm=tm)
        o = _attn_call(q, k, v, tq=tq, tk=tk)
        y_perm = _s5_call(u_perm, tabs, consts, l, rt=rt)
        x = _post_call(y_perm, u_perm, o, gates, x, ssm_d[l].reshape(1, SSM_WIDTH), wglu, b_glu[l][None],
                       wos, wom, wo, ffn_norm_g[l][None], w1, w2, l, tm=tm_post, fb=1024)
    return x
```

```python
import functools
import math

import jax
import jax.numpy as jnp
from jax import lax
from jax.experimental import pallas as pl
from jax.experimental.pallas import tpu as pltpu

D_MODEL = 1024
SSM_WIDTH = 512
SSM_GROUP = 16
SSM_GROUPS = 32
SSM_STATE = 64
MLA_HEADS = 8
QK_NOPE = 64
QK_ROPE = 32
QK_HEAD = QK_NOPE + QK_ROPE
V_HEAD = 64
Q_LORA = 384
KV_LORA = 256
MLA_WIDTH = MLA_HEADS * V_HEAD
ROPE_THETA = 10000.0
D_FF = 4 * D_MODEL
EPS = 1e-6

LANE = 128
HEAD_PAD = LANE
CHUNK = 8
GROUPS_PER_BLK = LANE // SSM_GROUP
N_GBLK = SSM_GROUPS // GROUPS_PER_BLK
STATE_BLK = 2 * 2 * GROUPS_PER_BLK * SSM_STATE
DIR_BLK = STATE_BLK // 2
SCAN_BLK = STATE_BLK // 4
VMEM_LIMIT = 56 * 1024 * 1024
Q_SCALE = QK_HEAD ** -0.5 * math.log2(math.e)

_C_U = 0
_C_CQ = _C_U + SSM_WIDTH
_C_CKV = _C_CQ + Q_LORA
_C_GATE = _C_CKV + KV_LORA
_C_KR = _C_GATE + 2 * D_MODEL
_C_KRS = _C_KR + HEAD_PAD
IN_COLS_PAD = _C_KRS + HEAD_PAD

bf16 = jnp.bfloat16
f32 = jnp.float32


def _dot(a, b):
    return jnp.dot(a, b, preferred_element_type=f32)


def _rms(x, g):
    return x * lax.rsqrt(jnp.mean(x * x, axis=-1, keepdims=True) + EPS) * g


def _pre_kernel(x_ref, ng_ref, win_ref, bg_ref, qng_ref, kvng_ref, wq_ref, wkv_ref,
                hg_ref, cos_ref, sin_ref,
                u_ref, gates_ref, q_ref, k_ref, v_ref, *, tm):
    x = x_ref[0]
    h = _rms(x, ng_ref[...]).astype(bf16)
    proj = _dot(h, win_ref[...])
    u = proj[:, _C_U:_C_CQ]
    for gb in range(N_GBLK):
        u_ref[:, 0, gb] = u[:, gb * LANE:(gb + 1) * LANE].reshape(tm // CHUNK, CHUNK, LANE)
    gates_ref[0] = jax.nn.sigmoid(proj[:, _C_GATE:_C_KR] + bg_ref[...]).astype(bf16)

    cq = _rms(proj[:, _C_CQ:_C_CKV], qng_ref[...]).astype(bf16)
    ckv = _rms(proj[:, _C_CKV:_C_GATE], kvng_ref[...]).astype(bf16)
    qq = _dot(cq, wq_ref[...])
    kk = _dot(ckv, wkv_ref[...])
    kr = proj[:, _C_KR:_C_KRS]
    krs = proj[:, _C_KRS:IN_COLS_PAD]
    cos = cos_ref[...]
    sin = sin_ref[...]
    hg = hg_ref[...]
    qc, qs = hg[0:1] * cos, hg[1:2] * sin
    kc, ks = hg[2:3] * cos, hg[3:4] * sin
    ks_term = krs * ks
    one_col = hg[4:5]
    hw = MLA_HEADS * HEAD_PAD
    for hd in range(MLA_HEADS):
        sl = slice(hd * HEAD_PAD, (hd + 1) * HEAD_PAD)
        sl2 = slice(hw + hd * HEAD_PAD, hw + (hd + 1) * HEAD_PAD)
        qh = qq[:, sl]
        r = lax.rsqrt(jnp.sum(qh * qh, axis=-1, keepdims=True) * (1.0 / QK_HEAD) + EPS) * Q_SCALE
        q_ref[0, hd] = (r * (qh * qc + qq[:, sl2] * qs)).astype(bf16)
        kh = kk[:, sl] + kr
        r = lax.rsqrt(jnp.sum(kh * kh, axis=-1, keepdims=True) * (1.0 / QK_HEAD) + EPS)
        k_ref[0, hd] = (r * (kh * kc + ks_term)).astype(bf16)
        v_ref[0, hd] = (kk[:, sl2] + one_col).astype(bf16)


def _layer_spec(shape, l):
    return pl.BlockSpec((None,) + tuple(shape), lambda b, i: (l,) + (0,) * len(shape))


def _pre_call(x, ng, win, bg, qng, kvng, wq, wkv, hg, cos_t, sin_t, l, *, tm):
    bsz, seq, _ = x.shape
    nch = seq // CHUNK
    const = lambda shape: pl.BlockSpec(shape, lambda b, i: (0,) * len(shape))
    return pl.pallas_call(
        functools.partial(_pre_kernel, tm=tm),
        grid=(bsz, seq // tm),
        in_specs=[
            pl.BlockSpec((1, tm, D_MODEL), lambda b, i: (b, i, 0)),
            const((1, D_MODEL)),
            _layer_spec((D_MODEL, IN_COLS_PAD), l),
            const((1, 2 * D_MODEL)),
            const((1, Q_LORA)),
            const((1, KV_LORA)),
            _layer_spec((Q_LORA, 2 * MLA_HEADS * HEAD_PAD), l),
            _layer_spec((KV_LORA, 2 * MLA_HEADS * HEAD_PAD), l),
            _layer_spec((8, HEAD_PAD), l),
            pl.BlockSpec((tm, HEAD_PAD), lambda b, i: (i, 0)),
            pl.BlockSpec((tm, HEAD_PAD), lambda b, i: (i, 0)),
        ],
        out_specs=[
            pl.BlockSpec((tm // CHUNK, 1, N_GBLK, CHUNK, LANE), lambda b, i: (i, b, 0, 0, 0)),
            pl.BlockSpec((1, tm, 2 * D_MODEL), lambda b, i: (b, i, 0)),
            pl.BlockSpec((1, MLA_HEADS, tm, HEAD_PAD), lambda b, i: (b, 0, i, 0)),
            pl.BlockSpec((1, MLA_HEADS, tm, HEAD_PAD), lambda b, i: (b, 0, i, 0)),
            pl.BlockSpec((1, MLA_HEADS, tm, HEAD_PAD), lambda b, i: (b, 0, i, 0)),
        ],
        out_shape=[
            jax.ShapeDtypeStruct((nch, bsz, N_GBLK, CHUNK, LANE), f32),
            jax.ShapeDtypeStruct((bsz, seq, 2 * D_MODEL), bf16),
            jax.ShapeDtypeStruct((bsz, MLA_HEADS, seq, HEAD_PAD), bf16),
            jax.ShapeDtypeStruct((bsz, MLA_HEADS, seq, HEAD_PAD), bf16),
            jax.ShapeDtypeStruct((bsz, MLA_HEADS, seq, HEAD_PAD), bf16),
        ],
        compiler_params=pltpu.CompilerParams(
            dimension_semantics=("parallel", "parallel"), vmem_limit_bytes=VMEM_LIMIT),
        name="pre",
    )(x, ng, win, bg, qng, kvng, wq, wkv, hg, cos_t, sin_t)


def _attn_kernel(q_ref, k_ref, v_ref, o_ref, s_scr, m_scr, acc_scr, *, tk):
    n_heads = q_ref.shape[1]
    nk = k_ref.shape[2] // tk
    reps = tk // LANE

    def qk_chunk(hd, j):
        slot = hd % 2
        kj = k_ref[0, hd, j * tk:(j + 1) * tk, :]
        s = lax.dot_general(q_ref[0, hd], kj, (((1,), (1,)), ((), ())), preferred_element_type=f32)
        s_scr[slot, j] = s
        m = s[:, :LANE] if j == 0 else m_scr[slot]
        for c in range(1 if j == 0 else 0, reps):
            m = jnp.maximum(m, s[:, c * LANE:(c + 1) * LANE])
        m_scr[slot] = m

    def pv_chunk(hd, j):
        slot = hd % 2
        p = jnp.exp2(s_scr[slot, j] - jnp.tile(m_scr[slot], (1, reps))).astype(bf16)
        pv = _dot(p, v_ref[0, hd, j * tk:(j + 1) * tk, :])
        acc_scr[...] = pv if j == 0 else acc_scr[...] + pv

    outs = []
    for stage in range(n_heads + 1):
        if stage > 0:
            slot = (stage - 1) % 2
            m_scr[slot] = jnp.broadcast_to(jnp.max(m_scr[slot], axis=-1, keepdims=True), m_scr.shape[1:])
        for j in range(nk):
            if stage < n_heads:
                qk_chunk(stage, j)
            if stage > 0:
                pv_chunk(stage - 1, j)
        if stage > 0:
            acc = acc_scr[...]
            outs.append(acc[:, :V_HEAD] / acc[:, V_HEAD:V_HEAD + 1])
    o_ref[0] = jnp.concatenate(outs, axis=-1).astype(o_ref.dtype)


def _attn_call(q, k, v, *, tq, tk):
    bsz, nh, seq, _ = q.shape
    return pl.pallas_call(
        functools.partial(_attn_kernel, tk=tk),
        grid=(bsz, nh // 2, seq // tq),
        in_specs=[
            pl.BlockSpec((1, 2, tq, HEAD_PAD), lambda b, h, i: (b, h, i, 0)),
            pl.BlockSpec((1, 2, seq, HEAD_PAD), lambda b, h, i: (b, h, 0, 0)),
            pl.BlockSpec((1, 2, seq, HEAD_PAD), lambda b, h, i: (b, h, 0, 0)),
        ],
        out_specs=pl.BlockSpec((1, tq, 2 * V_HEAD), lambda b, h, i: (b, i, h)),
        out_shape=jax.ShapeDtypeStruct((bsz, seq, MLA_WIDTH), bf16),
        scratch_shapes=[pltpu.VMEM((2, seq // tk, tq, tk), f32), pltpu.VMEM((2, tq, LANE), f32),
                        pltpu.VMEM((tq, HEAD_PAD), f32)],
        compiler_params=pltpu.CompilerParams(
            dimension_semantics=("parallel", "parallel", "parallel"), vmem_limit_bytes=VMEM_LIMIT),
        name="attn",
    )(q, k, v)


def _chunk_rows(u_ref):
    return jnp.concatenate([u_ref[:, t, :] for t in range(CHUNK)], axis=-1).astype(bf16)


def _spread(compact_ref, spread_ref, mask_ref):
    return (_dot(compact_ref[...], spread_ref[...]) * mask_ref[...]).astype(bf16)


def _store_chunks(y_ref, y):
    for t in range(CHUNK):
        y_ref[:, t, :] = y[:, t * LANE:(t + 1) * LANE]


def _s5_dir_kernel(u_ref, bc_ref, cc_ref, mc_ref, ar_ref, ai_ref, spread_st_ref, spread_io_ref,
                   mask_b_ref, mask_c_ref, mask_m_ref, y_ref, bw_scr, cw_scr, mw_scr, s_scr, carry_scr, *, nb):
    bwd = pl.program_id(1)
    step = pl.program_id(2)
    rows = 2 * nb
    q = SCAN_BLK // 2

    @pl.when(step == 0)
    def _():
        bw_scr[...] = _spread(bc_ref, spread_st_ref, mask_b_ref)
        cw_scr[...] = _spread(cc_ref, spread_io_ref, mask_c_ref)
        carry_scr[...] = jnp.zeros(carry_scr.shape, f32)

    @pl.when(jnp.logical_and(step == 0, bwd == 0))
    def _():
        mw_scr[...] = _spread(mc_ref, spread_io_ref, mask_m_ref)

    x = _chunk_rows(u_ref)
    s_scr[...] = _dot(x, bw_scr[...])

    n_tiles = s_scr.shape[0] // rows
    low = lax.broadcasted_iota(jnp.int32, (rows, q), 0) < nb
    first = jnp.logical_xor(low, bwd == 1)
    halves = [(slice(hf * SCAN_BLK, hf * SCAN_BLK + q), slice(hf * SCAN_BLK + q, (hf + 1) * SCAN_BLK))
              for hf in range(2)]
    decay = []
    for hf in range(2):
        ar = jnp.broadcast_to(ar_ref[:, hf * q:(hf + 1) * q], (rows, q))
        ai = jnp.broadcast_to(ai_ref[:, hf * q:(hf + 1) * q], (rows, q))
        decay.append((ar, ai, ar * ar - ai * ai, 2.0 * ar * ai))

    def swap(a):
        return pltpu.roll(a, nb, axis=0)

    def body(i, carry):
        t = jnp.where(bwd == 1, n_tiles - 1 - i, i)
        row = pl.multiple_of(t * rows, rows)
        out = []
        for (re, im), (ar, ai, a2r, a2i), (hr, hi) in zip(halves, decay, carry):
            sr = s_scr[pl.ds(row, rows), re]
            si = s_scr[pl.ds(row, rows), im]
            xr, xi = swap(sr), swap(si)
            fr, fi = jnp.where(first, sr, xr), jnp.where(first, si, xi)
            gr, gi = jnp.where(first, xr, sr), jnp.where(first, xi, si)
            mr, mi = ar * hr - ai * hi + fr, ar * hi + ai * hr + fi
            s_scr[pl.ds(row, rows), re] = jnp.where(first, hr, mr)
            s_scr[pl.ds(row, rows), im] = jnp.where(first, hi, mi)
            tr, ti = ar * fr - ai * fi + gr, ar * fi + ai * fr + gi
            out.append((a2r * hr - a2i * hi + tr, a2r * hi + a2i * hr + ti))
        return tuple(out)

    init = tuple((carry_scr[:, re], carry_scr[:, im]) for re, im in halves)
    final = lax.fori_loop(0, n_tiles, body, init, unroll=2)
    for (re, im), (hr, hi) in zip(halves, final):
        carry_scr[:, re] = hr
        carry_scr[:, im] = hi

    h = s_scr[...].astype(bf16)

    @pl.when(bwd == 0)
    def _():
        _store_chunks(y_ref, _dot(h, cw_scr[...]) + _dot(x, mw_scr[...]))

    @pl.when(bwd == 1)
    def _():
        _store_chunks(y_ref, _dot(h, cw_scr[...]))


def _s5_spread_constants():
    L, P, N, GB = CHUNK, SSM_GROUP, SSM_STATE, GROUPS_PER_BLK
    io_lane = jnp.arange(L * LANE)
    io_small = jnp.arange(L * P)
    st_lane = jnp.arange(DIR_BLK)
    st_small = jnp.arange(2 * N)
    spread_io = ((io_small[:, None] // P == io_lane[None, :] // LANE)
                 & (io_small[:, None] % P == io_lane[None, :] % P))
    spread_st = ((st_small[:, None] // N == (st_lane[None, :] // (4 * N)) % 2)
                 & (st_small[:, None] % N == st_lane[None, :] % N))
    grp_io = (io_lane // P) % GB
    grp_st = (st_lane // SCAN_BLK) * (GB // 2) + (st_lane // N) % (GB // 2)
    mask_m = grp_io[:, None] == grp_io[None, :]
    mask_b = grp_io[:, None] == grp_st[None, :]
    mask_c = grp_st[:, None] == grp_io[None, :]
    return tuple(a.astype(bf16) for a in (spread_io, spread_st, mask_m, mask_b, mask_c))


def _s5_call(u_perm, tabs, consts, l, *, rt):
    nch, bsz = u_perm.shape[0], u_perm.shape[1]
    kblk = CHUNK * LANE
    rows, nt = rt * bsz, nch // rt
    u2 = u_perm.reshape(nch * bsz, N_GBLK, CHUNK, LANE)
    mc, bc, cc, a_re, a_im = tabs
    spread_io, spread_st, mask_m, mask_b, mask_c = consts
    small_io, small_st = CHUNK * SSM_GROUP, 2 * SSM_STATE
    tile = lambda d, i: i + d * (nt - 1 - 2 * i)
    whole = lambda a: pl.BlockSpec(a.shape, lambda g, d, i: (0,) * a.ndim)
    y = pl.pallas_call(
        functools.partial(_s5_dir_kernel, nb=bsz),
        grid=(N_GBLK, 2, nt),
        in_specs=[
            pl.BlockSpec((rows, None, CHUNK, LANE), lambda g, d, i: (tile(d, i), g, 0, 0)),
            pl.BlockSpec((None, None, kblk, small_st), lambda g, d, i: (l, g, 0, d)),
            pl.BlockSpec((None, None, DIR_BLK, small_io), lambda g, d, i: (l, g, d, 0)),
            pl.BlockSpec((None, None, kblk, small_io), lambda g, d, i: (l, g, 0, 0)),
            pl.BlockSpec((None, 1, SCAN_BLK), lambda g, d, i: (l, 0, 2 * g + d)),
            pl.BlockSpec((None, 1, SCAN_BLK), lambda g, d, i: (l, 0, 2 * g + d)),
            whole(spread_st), whole(spread_io), whole(mask_b), whole(mask_c), whole(mask_m),
        ],
        out_specs=pl.BlockSpec((None, rows, None, CHUNK, LANE), lambda g, d, i: (d, tile(d, i), g, 0, 0)),
        out_shape=jax.ShapeDtypeStruct((2, nch * bsz, N_GBLK, CHUNK, LANE), f32),
        scratch_shapes=[pltpu.VMEM((kblk, DIR_BLK), bf16), pltpu.VMEM((DIR_BLK, kblk), bf16),
                        pltpu.VMEM((kblk, kblk), bf16), pltpu.VMEM((rows, DIR_BLK), f32),
                        pltpu.VMEM((2 * bsz, DIR_BLK), f32)],
        compiler_params=pltpu.CompilerParams(
            dimension_semantics=("arbitrary", "arbitrary", "arbitrary"), vmem_limit_bytes=VMEM_LIMIT),
        name="s5",
    )(u2, bc, cc, mc, a_re, a_im, spread_st, spread_io, mask_b, mask_c, mask_m)
    return y.reshape(2, nch, bsz, N_GBLK, CHUNK, LANE)


def _s5_tables(lam_re, lam_im, log_step, b_re, b_im, c_re, c_im):
    L, G, N, P = CHUNK, SSM_GROUPS, SSM_STATE, SSM_GROUP
    hp = lax.Precision.HIGH
    step = jnp.exp(log_step)[..., None]
    zr, zi = lam_re * step, lam_im * step
    mag = jnp.exp(zr)
    abar_r, abar_i = mag * jnp.cos(zi), mag * jnp.sin(zi)
    nr, ni = abar_r - 1.0, abar_i
    den = lam_re * lam_re + lam_im * lam_im
    fr = (nr * lam_re + ni * lam_im) / den
    fi = (ni * lam_re - nr * lam_im) / den
    bbr = fr[..., None] * b_re - fi[..., None] * b_im
    bbi = fr[..., None] * b_im + fi[..., None] * b_re
    j = jnp.arange(L + 1, dtype=f32)[:, None, None, None]
    pmag = jnp.exp(j * zr)
    pr, pi = pmag * jnp.cos(j * zi), pmag * jnp.sin(j * zi)

    cpr = c_re[None, None] * pr[:L, :, :, None, :] - c_im[None, None] * pi[:L, :, :, None, :]
    cpi = c_re[None, None] * pi[:L, :, :, None, :] + c_im[None, None] * pr[:L, :, :, None, :]
    kern = (jnp.einsum('jdgpn,dgnq->jdgpq', cpr, bbr, precision=hp)
            - jnp.einsum('jdgpn,dgnq->jdgpq', cpi, bbi, precision=hp))
    s_idx = jnp.arange(L)[:, None]
    t_idx = jnp.arange(L)[None, :]
    kf = kern[:, 0][jnp.clip(t_idx - s_idx, 0, L - 1)] * (s_idx <= t_idx)[:, :, None, None, None]
    kb = kern[:, 1][jnp.clip(s_idx - t_idx, 0, L - 1)] * (s_idx >= t_idx)[:, :, None, None, None]
    mg = jnp.transpose(kf + kb, (2, 0, 4, 1, 3))
    GB = GROUPS_PER_BLK
    mc = jnp.transpose(mg.reshape(N_GBLK, GB, L, P, L, P), (0, 2, 1, 3, 4, 5)).reshape(N_GBLK, L * LANE, L * P)

    psr = jnp.stack([pr[:L, 0][::-1], pr[:L, 1]])
    psi = jnp.stack([pi[:L, 0][::-1], pi[:L, 1]])
    wr = psr[..., None] * bbr[:, None] - psi[..., None] * bbi[:, None]
    wi = psr[..., None] * bbi[:, None] + psi[..., None] * bbr[:, None]
    wst = jnp.stack([wr, wi]).reshape(2, 2, L, N_GBLK, GB, N, P)
    bc = jnp.transpose(wst, (3, 2, 4, 6, 1, 0, 5)).reshape(N_GBLK, L * LANE, 4 * N)

    pcr = jnp.stack([pr[1:, 0], pr[1:, 1][::-1]])
    pci = jnp.stack([pi[1:, 0], pi[1:, 1][::-1]])
    ccr = c_re[None, None] * pcr[:, :, :, None, :] - c_im[None, None] * pci[:, :, :, None, :]
    cci = c_re[None, None] * pci[:, :, :, None, :] + c_im[None, None] * pcr[:, :, :, None, :]
    cst = jnp.stack([ccr, -cci]).reshape(2, 2, L, N_GBLK, 2, GB // 2, P, N)
    cc = jnp.transpose(cst, (3, 1, 4, 0, 5, 7, 2, 6)).reshape(N_GBLK, STATE_BLK, L * P)

    def scan_order(a):
        a = a.reshape(2, N_GBLK, 2, GB // 2, N)
        return jnp.transpose(a, (1, 0, 2, 3, 4)).reshape(1, -1)

    return mc.astype(bf16), bc.astype(bf16), cc.astype(bf16), scan_order(pr[L]), scan_order(pi[L])


def _mix_kernel(yf_ref, yb_ref, u_ref, o_ref, gates_ref, x_ref, d_ref, wglu_ref, bglu_ref,
                wos_ref, wom_ref, wo_ref, out_ref, *, tm):
    parts = []
    for gb in range(N_GBLK):
        sl = slice(gb * LANE, (gb + 1) * LANE)
        yg = (yf_ref[:, 0, gb] + yb_ref[:, 0, gb]).reshape(tm, LANE)
        ug = u_ref[:, 0, gb].reshape(tm, LANE)
        parts.append(yg + d_ref[:, sl] * ug)
    y = jax.nn.gelu(jnp.concatenate(parts, axis=-1))
    y = y * jax.nn.sigmoid(_dot(y.astype(bf16), wglu_ref[...]) + bglu_ref[...])
    a = _dot(y.astype(bf16), wos_ref[...])
    m = _dot(o_ref[0], wom_ref[...])
    gates = gates_ref[0].astype(f32)
    merged = gates[:, :D_MODEL] * a + gates[:, D_MODEL:] * m
    out_ref[0] = x_ref[0] + _dot(merged.astype(bf16), wo_ref[...])


def _mix_call(y_perm, u_perm, o, gates, x, d, wglu, bglu, wos, wom, wo, l, *, tm):
    bsz, seq, _ = x.shape
    const = lambda shape: pl.BlockSpec(shape, lambda b, i: (0,) * len(shape))
    perm_spec = pl.BlockSpec((tm // CHUNK, 1, N_GBLK, CHUNK, LANE), lambda b, i: (i, b, 0, 0, 0))
    dir_spec = lambda d: pl.BlockSpec((None, tm // CHUNK, 1, N_GBLK, CHUNK, LANE), lambda b, i: (d, i, b, 0, 0, 0))
    return pl.pallas_call(
        functools.partial(_mix_kernel, tm=tm),
        grid=(bsz, seq // tm),
        in_specs=[
            dir_spec(0), dir_spec(1), perm_spec,
            pl.BlockSpec((1, tm, MLA_WIDTH), lambda b, i: (b, i, 0)),
            pl.BlockSpec((1, tm, 2 * D_MODEL), lambda b, i: (b, i, 0)),
            pl.BlockSpec((1, tm, D_MODEL), lambda b, i: (b, i, 0)),
            const((1, SSM_WIDTH)),
            _layer_spec((SSM_WIDTH, SSM_WIDTH), l),
            const((1, SSM_WIDTH)),
            _layer_spec((SSM_WIDTH, D_MODEL), l),
            _layer_spec((MLA_WIDTH, D_MODEL), l),
            _layer_spec((D_MODEL, D_MODEL), l),
        ],
        out_specs=pl.BlockSpec((1, tm, D_MODEL), lambda b, i: (b, i, 0)),
        out_shape=jax.ShapeDtypeStruct(x.shape, f32),
        compiler_params=pltpu.CompilerParams(
            dimension_semantics=("parallel", "parallel"), vmem_limit_bytes=VMEM_LIMIT),
        name="mix",
    )(y_perm, y_perm, u_perm, o, gates, x, d, wglu, bglu, wos, wom, wo)


def _ffn_kernel(x_ref, g_ref, w1_ref, w2_ref, out_ref, *, fb):
    x = x_ref[...]
    h = _rms(x, g_ref[...]).astype(bf16)
    acc = x
    for j in range(D_FF // fb):
        a = jnp.maximum(_dot(h, w1_ref[:, j * fb:(j + 1) * fb]), 0.0)
        acc = acc + _dot((a * a).astype(bf16), w2_ref[j * fb:(j + 1) * fb, :])
    out_ref[...] = acc


def _ffn_call(x2, g, w1, w2, l, *, tm, fb):
    ntok = x2.shape[0]
    return pl.pallas_call(
        functools.partial(_ffn_kernel, fb=fb),
        grid=(ntok // tm,),
        in_specs=[
            pl.BlockSpec((tm, D_MODEL), lambda i: (i, 0)),
            pl.BlockSpec((1, D_MODEL), lambda i: (0, 0)),
            pl.BlockSpec((None, D_MODEL, D_FF), lambda i: (l, 0, 0)),
            pl.BlockSpec((None, D_FF, D_MODEL), lambda i: (l, 0, 0)),
        ],
        out_specs=pl.BlockSpec((tm, D_MODEL), lambda i: (i, 0)),
        out_shape=jax.ShapeDtypeStruct(x2.shape, f32),
        compiler_params=pltpu.CompilerParams(dimension_semantics=("parallel",), vmem_limit_bytes=VMEM_LIMIT),
        name="ffn",
    )(x2, g, w1, w2)


def _rope_swap(a):
    half = QK_ROPE // 2
    return jnp.concatenate([a[..., half:], a[..., :half]], axis=-1)


def _place_rope(a):
    pad = [(0, 0)] * (a.ndim - 1) + [(QK_NOPE, HEAD_PAD - QK_HEAD)]
    return jnp.pad(a, pad)


def _pad_head(a, width):
    pad = [(0, 0)] * (a.ndim - 1) + [(0, HEAD_PAD - width)]
    a = jnp.pad(a, pad)
    return a.reshape(a.shape[:-2] + (MLA_HEADS * HEAD_PAD,))


def _layer_params(w, w_q, w_kv, qg, kg):
    o3 = SSM_WIDTH + Q_LORA + KV_LORA
    o4 = o3 + QK_ROPE
    wkr = w[:, o3:o4]
    win = jnp.concatenate([w[:, :o3], w[:, o4:], _place_rope(wkr), _place_rope(_rope_swap(wkr))], axis=1).astype(bf16)
    wq = w_q.reshape(Q_LORA, MLA_HEADS, QK_HEAD)
    wq_sw = _place_rope(_rope_swap(wq[..., QK_NOPE:])).reshape(Q_LORA, MLA_HEADS * HEAD_PAD)
    wq = jnp.concatenate([_pad_head(wq, QK_HEAD), wq_sw], axis=1).astype(bf16)
    wkv = w_kv.reshape(KV_LORA, MLA_HEADS, QK_NOPE + V_HEAD)
    wkv = jnp.concatenate([_pad_head(wkv[..., :QK_NOPE], QK_NOPE), _pad_head(wkv[..., QK_NOPE:], V_HEAD)],
                          axis=1).astype(bf16)
    one_col = jnp.zeros((HEAD_PAD,), f32).at[V_HEAD].set(1.0)
    zero = jnp.zeros((HEAD_PAD,), f32)
    hg = jnp.stack([
        jnp.pad(qg, (0, HEAD_PAD - QK_HEAD)), _place_rope(_rope_swap(qg[QK_NOPE:])),
        jnp.pad(kg, (0, HEAD_PAD - QK_HEAD)), _place_rope(_rope_swap(kg[QK_NOPE:])),
        one_col, zero, zero, zero])
    return win, wq, wkv, hg


def _rope_tables(seq):
    half = QK_ROPE // 2
    inv_freq = ROPE_THETA ** (-jnp.arange(half, dtype=f32) / half)
    ang = jnp.arange(seq, dtype=f32)[:, None] * inv_freq[None, :]
    cos, sin = jnp.cos(ang), jnp.sin(ang)
    ones = jnp.ones((seq, QK_NOPE), f32)
    zpad = jnp.zeros((seq, HEAD_PAD - QK_HEAD), f32)
    cos_t = jnp.concatenate([ones, cos, cos, zpad], axis=1)
    sin_t = jnp.concatenate([0.0 * ones, -sin, sin, zpad], axis=1)
    return cos_t, sin_t


def kernel(x, mix_norm_g, w_in, b_gate, ssm_lam_re, ssm_lam_im, ssm_log_step, ssm_b_re, ssm_b_im, ssm_c_re, ssm_c_im, ssm_d, w_glu, b_glu, w_out_ssm, q_norm_g, kv_norm_g, w_q_up, w_kv_up, q_head_g, k_head_g, w_out_mla, w_o, ffn_norm_g, w_ff1, w_ff2):
    bsz, seq, _ = x.shape
    depth = w_in.shape[0]
    nch = seq // CHUNK
    tm = min(512, seq)
    tq = min(512, seq)
    tk = min(4096, seq)
    rt = min(256, nch)
    assert 2 * bsz == 8, "the S5 chunk scan packs two chunks of `bsz` batch rows into one 8-sublane tile"
    cos_t, sin_t = _rope_tables(seq)
    win, wq, wkv, hg = jax.vmap(_layer_params)(w_in, w_q_up, w_kv_up, q_head_g, k_head_g)
    tabs = jax.vmap(_s5_tables)(ssm_lam_re, ssm_lam_im, ssm_log_step, ssm_b_re, ssm_b_im, ssm_c_re, ssm_c_im)
    consts = _s5_spread_constants()
    wglu, wos, wom, wo, w1, w2 = (a.astype(bf16) for a in (w_glu, w_out_ssm, w_out_mla, w_o, w_ff1, w_ff2))
    for l in range(depth):
        u_perm, gates, q, k, v = _pre_call(
            x, mix_norm_g[l][None], win, b_gate[l].reshape(1, 2 * D_MODEL), q_norm_g[l][None],
            kv_norm_g[l][None], wq, wkv, hg, cos_t, sin_t, l, tm=tm)
        o = _attn_call(q, k, v, tq=tq, tk=tk)
        y_perm = _s5_call(u_perm, tabs, consts, l, rt=rt)
        x = _mix_call(y_perm, u_perm, o, gates, x, ssm_d[l].reshape(1, SSM_WIDTH), wglu, b_glu[l][None],
                      wos, wom, wo, l, tm=tm)
        x = _ffn_call(x.reshape(bsz * seq, D_MODEL), ffn_norm_g[l][None], w1, w2, l,
                      tm=tm, fb=1024).reshape(bsz, seq, D_MODEL)
    return x
```

```python
import functools
import math

import jax
import jax.numpy as jnp
from jax import lax
from jax.experimental import pallas as pl
from jax.experimental.pallas import tpu as pltpu

D_MODEL = 1024
SSM_WIDTH = 512
SSM_GROUP = 16
SSM_GROUPS = 32
SSM_STATE = 64
MLA_HEADS = 8
QK_NOPE = 64
QK_ROPE = 32
QK_HEAD = QK_NOPE + QK_ROPE
V_HEAD = 64
Q_LORA = 384
KV_LORA = 256
MLA_WIDTH = MLA_HEADS * V_HEAD
ROPE_THETA = 10000.0
D_FF = 4 * D_MODEL
EPS = 1e-6

LANE = 128
HEAD_PAD = LANE
CHUNK = 8
GROUPS_PER_BLK = LANE // SSM_GROUP
N_GBLK = SSM_GROUPS // GROUPS_PER_BLK
STATE_BLK = 2 * 2 * GROUPS_PER_BLK * SSM_STATE
SCAN_BLK = STATE_BLK // 4
VMEM_LIMIT = 56 * 1024 * 1024
Q_SCALE = QK_HEAD ** -0.5 * math.log2(math.e)

_C_U = 0
_C_CQ = _C_U + SSM_WIDTH
_C_CKV = _C_CQ + Q_LORA
_C_GATE = _C_CKV + KV_LORA
_C_KR = _C_GATE + 2 * D_MODEL
_C_KRS = _C_KR + HEAD_PAD
IN_COLS_PAD = _C_KRS + HEAD_PAD

bf16 = jnp.bfloat16
f32 = jnp.float32


def _dot(a, b):
    return jnp.dot(a, b, preferred_element_type=f32)


def _rms(x, g):
    return x * lax.rsqrt(jnp.mean(x * x, axis=-1, keepdims=True) + EPS) * g


def _pre_kernel(x_ref, ng_ref, win_ref, bg_ref, qng_ref, kvng_ref, wq_ref, wkv_ref,
                hg_ref, cos_ref, sin_ref,
                u_ref, gates_ref, q_ref, k_ref, v_ref, *, tm):
    x = x_ref[0]
    h = _rms(x, ng_ref[...]).astype(bf16)
    proj = _dot(h, win_ref[...])
    u = proj[:, _C_U:_C_CQ]
    for gb in range(N_GBLK):
        u_ref[:, 0, gb] = u[:, gb * LANE:(gb + 1) * LANE].reshape(tm // CHUNK, CHUNK, LANE)
    gates_ref[0] = jax.nn.sigmoid(proj[:, _C_GATE:_C_KR] + bg_ref[...]).astype(bf16)

    cq = _rms(proj[:, _C_CQ:_C_CKV], qng_ref[...]).astype(bf16)
    ckv = _rms(proj[:, _C_CKV:_C_GATE], kvng_ref[...]).astype(bf16)
    qq = _dot(cq, wq_ref[...])
    kk = _dot(ckv, wkv_ref[...])
    kr = proj[:, _C_KR:_C_KRS]
    krs = proj[:, _C_KRS:IN_COLS_PAD]
    cos = cos_ref[...]
    sin = sin_ref[...]
    hg = hg_ref[...]
    qc, qs = hg[0:1] * cos, hg[1:2] * sin
    kc, ks = hg[2:3] * cos, hg[3:4] * sin
    ks_term = krs * ks
    one_col = hg[4:5]
    hw = MLA_HEADS * HEAD_PAD
    for hd in range(MLA_HEADS):
        sl = slice(hd * HEAD_PAD, (hd + 1) * HEAD_PAD)
        sl2 = slice(hw + hd * HEAD_PAD, hw + (hd + 1) * HEAD_PAD)
        qh = qq[:, sl]
        r = lax.rsqrt(jnp.sum(qh * qh, axis=-1, keepdims=True) * (1.0 / QK_HEAD) + EPS) * Q_SCALE
        q_ref[0, hd] = (r * (qh * qc + qq[:, sl2] * qs)).astype(bf16)
        kh = kk[:, sl] + kr
        r = lax.rsqrt(jnp.sum(kh * kh, axis=-1, keepdims=True) * (1.0 / QK_HEAD) + EPS)
        k_ref[0, hd] = (r * (kh * kc + ks_term)).astype(bf16)
        v_ref[0, hd] = (kk[:, sl2] + one_col).astype(bf16)


def _layer_spec(shape, l):
    return pl.BlockSpec((None,) + tuple(shape), lambda b, i: (l,) + (0,) * len(shape))


def _pre_call(x, ng, win, bg, qng, kvng, wq, wkv, hg, cos_t, sin_t, l, *, tm):
    bsz, seq, _ = x.shape
    nch = seq // CHUNK
    const = lambda shape: pl.BlockSpec(shape, lambda b, i: (0,) * len(shape))
    return pl.pallas_call(
        functools.partial(_pre_kernel, tm=tm),
        grid=(bsz, seq // tm),
        in_specs=[
            pl.BlockSpec((1, tm, D_MODEL), lambda b, i: (b, i, 0)),
            const((1, D_MODEL)),
            _layer_spec((D_MODEL, IN_COLS_PAD), l),
            const((1, 2 * D_MODEL)),
            const((1, Q_LORA)),
            const((1, KV_LORA)),
            _layer_spec((Q_LORA, 2 * MLA_HEADS * HEAD_PAD), l),
            _layer_spec((KV_LORA, 2 * MLA_HEADS * HEAD_PAD), l),
            _layer_spec((8, HEAD_PAD), l),
            pl.BlockSpec((tm, HEAD_PAD), lambda b, i: (i, 0)),
            pl.BlockSpec((tm, HEAD_PAD), lambda b, i: (i, 0)),
        ],
        out_specs=[
            pl.BlockSpec((tm // CHUNK, 1, N_GBLK, CHUNK, LANE), lambda b, i: (i, b, 0, 0, 0)),
            pl.BlockSpec((1, tm, 2 * D_MODEL), lambda b, i: (b, i, 0)),
            pl.BlockSpec((1, MLA_HEADS, tm, HEAD_PAD), lambda b, i: (b, 0, i, 0)),
            pl.BlockSpec((1, MLA_HEADS, tm, HEAD_PAD), lambda b, i: (b, 0, i, 0)),
            pl.BlockSpec((1, MLA_HEADS, tm, HEAD_PAD), lambda b, i: (b, 0, i, 0)),
        ],
        out_shape=[
            jax.ShapeDtypeStruct((nch, bsz, N_GBLK, CHUNK, LANE), f32),
            jax.ShapeDtypeStruct((bsz, seq, 2 * D_MODEL), bf16),
            jax.ShapeDtypeStruct((bsz, MLA_HEADS, seq, HEAD_PAD), bf16),
            jax.ShapeDtypeStruct((bsz, MLA_HEADS, seq, HEAD_PAD), bf16),
            jax.ShapeDtypeStruct((bsz, MLA_HEADS, seq, HEAD_PAD), bf16),
        ],
        compiler_params=pltpu.CompilerParams(
            dimension_semantics=("parallel", "parallel"), vmem_limit_bytes=VMEM_LIMIT),
        name="pre",
    )(x, ng, win, bg, qng, kvng, wq, wkv, hg, cos_t, sin_t)


def _attn_kernel(q_ref, k_ref, v_ref, o_ref, s_scr, m_scr, acc_scr, *, tk):
    n_heads = q_ref.shape[1]
    nk = k_ref.shape[2] // tk
    reps = tk // LANE

    def qk_chunk(hd, j):
        slot = hd % 2
        kj = k_ref[0, hd, j * tk:(j + 1) * tk, :]
        s = lax.dot_general(q_ref[0, hd], kj, (((1,), (1,)), ((), ())), preferred_element_type=f32)
        s_scr[slot, j] = s
        m = s[:, :LANE] if j == 0 else m_scr[slot]
        for c in range(1 if j == 0 else 0, reps):
            m = jnp.maximum(m, s[:, c * LANE:(c + 1) * LANE])
        m_scr[slot] = m

    def pv_chunk(hd, j):
        slot = hd % 2
        p = jnp.exp2(s_scr[slot, j] - jnp.tile(m_scr[slot], (1, reps))).astype(bf16)
        pv = _dot(p, v_ref[0, hd, j * tk:(j + 1) * tk, :])
        acc_scr[...] = pv if j == 0 else acc_scr[...] + pv

    outs = []
    for stage in range(n_heads + 1):
        if stage > 0:
            slot = (stage - 1) % 2
            m_scr[slot] = jnp.broadcast_to(jnp.max(m_scr[slot], axis=-1, keepdims=True), m_scr.shape[1:])
        for j in range(nk):
            if stage < n_heads:
                qk_chunk(stage, j)
            if stage > 0:
                pv_chunk(stage - 1, j)
        if stage > 0:
            acc = acc_scr[...]
            outs.append(acc[:, :V_HEAD] / acc[:, V_HEAD:V_HEAD + 1])
    o_ref[0] = jnp.concatenate(outs, axis=-1).astype(o_ref.dtype)


def _attn_call(q, k, v, *, tq, tk):
    bsz, nh, seq, _ = q.shape
    return pl.pallas_call(
        functools.partial(_attn_kernel, tk=tk),
        grid=(bsz, nh // 2, seq // tq),
        in_specs=[
            pl.BlockSpec((1, 2, tq, HEAD_PAD), lambda b, h, i: (b, h, i, 0)),
            pl.BlockSpec((1, 2, seq, HEAD_PAD), lambda b, h, i: (b, h, 0, 0)),
            pl.BlockSpec((1, 2, seq, HEAD_PAD), lambda b, h, i: (b, h, 0, 0)),
        ],
        out_specs=pl.BlockSpec((1, tq, 2 * V_HEAD), lambda b, h, i: (b, i, h)),
        out_shape=jax.ShapeDtypeStruct((bsz, seq, MLA_WIDTH), bf16),
        scratch_shapes=[pltpu.VMEM((2, seq // tk, tq, tk), f32), pltpu.VMEM((2, tq, LANE), f32),
                        pltpu.VMEM((tq, HEAD_PAD), f32)],
        compiler_params=pltpu.CompilerParams(
            dimension_semantics=("parallel", "parallel", "parallel"), vmem_limit_bytes=VMEM_LIMIT),
        name="attn",
    )(q, k, v)


def _chunk_rows(u_ref):
    return jnp.concatenate([u_ref[:, t, :] for t in range(CHUNK)], axis=-1).astype(bf16)


def _spread(compact_ref, spread_ref, mask_ref):
    return (_dot(compact_ref[...], spread_ref[...]) * mask_ref[...]).astype(bf16)


def _s5_state_kernel(u_ref, bc_ref, spread_ref, mask_ref, s_ref, w_scr):
    @pl.when(pl.program_id(1) == 0)
    def _():
        w_scr[...] = _spread(bc_ref, spread_ref, mask_ref)

    s_ref[...] = _dot(_chunk_rows(u_ref), w_scr[...])


def _s5_scan_kernel(s_ref, ar_ref, ai_ref, h_ref, *, nch, nb):
    half = SCAN_BLK // 2
    rows = 2 * nb
    bwd = (pl.program_id(0) // 2) % 2
    ar = jnp.broadcast_to(ar_ref[...], (rows, half))
    ai = jnp.broadcast_to(ai_ref[...], (rows, half))
    a2r, a2i = ar * ar - ai * ai, 2.0 * ar * ai
    low = lax.broadcasted_iota(jnp.int32, (rows, half), 0) < nb
    first = jnp.logical_xor(low, bwd == 1)

    def swap(a):
        return pltpu.roll(a, nb, axis=0)

    def body(i, carry):
        hr, hi = carry
        t = jnp.where(bwd == 1, nch // 2 - 1 - i, i)
        row = pl.multiple_of(t * rows, rows)
        sr = s_ref[pl.ds(row, rows), 0:half]
        si = s_ref[pl.ds(row, rows), half:SCAN_BLK]
        xr, xi = swap(sr), swap(si)
        fr, fi = jnp.where(first, sr, xr), jnp.where(first, si, xi)
        gr, gi = jnp.where(first, xr, sr), jnp.where(first, xi, si)
        mr, mi = ar * hr - ai * hi + fr, ar * hi + ai * hr + fi
        h_ref[pl.ds(row, rows), 0:half] = jnp.where(first, hr, mr)
        h_ref[pl.ds(row, rows), half:SCAN_BLK] = jnp.where(first, hi, mi)
        tr, ti = ar * fr - ai * fi + gr, ar * fi + ai * fr + gi
        return a2r * hr - a2i * hi + tr, a2r * hi + a2i * hr + ti

    zero = jnp.zeros((rows, half), f32)
    lax.fori_loop(0, nch // 2, body, (zero, zero), unroll=4)


def _s5_out_kernel(u_ref, h_ref, mc_ref, cc_ref, spread_ref, mask_m_ref, mask_c_ref, y_ref, m_scr, c_scr):
    @pl.when(pl.program_id(1) == 0)
    def _():
        m_scr[...] = _spread(mc_ref, spread_ref, mask_m_ref)
        c_scr[...] = _spread(cc_ref, spread_ref, mask_c_ref)

    y = _dot(_chunk_rows(u_ref), m_scr[...]) + _dot(h_ref[...].astype(bf16), c_scr[...])
    for t in range(CHUNK):
        y_ref[:, t, :] = y[:, t * LANE:(t + 1) * LANE]


def _s5_spread_constants():
    L, P, N, GB = CHUNK, SSM_GROUP, SSM_STATE, GROUPS_PER_BLK
    io_lane = jnp.arange(L * LANE)
    io_small = jnp.arange(L * P)
    st_lane = jnp.arange(STATE_BLK)
    st_small = jnp.arange(4 * N)
    spread_io = ((io_small[:, None] // P == io_lane[None, :] // LANE)
                 & (io_small[:, None] % P == io_lane[None, :] % P))
    spread_st = ((st_small[:, None] // (2 * N) == st_lane[None, :] // (STATE_BLK // 2))
                 & ((st_small[:, None] // N) % 2 == (st_lane[None, :] // (4 * N)) % 2)
                 & (st_small[:, None] % N == st_lane[None, :] % N))
    grp_io = (io_lane // P) % GB
    grp_st = ((st_lane // (STATE_BLK // 4)) % 2) * (GB // 2) + (st_lane // N) % (GB // 2)
    mask_m = grp_io[:, None] == grp_io[None, :]
    mask_b = grp_io[:, None] == grp_st[None, :]
    mask_c = grp_st[:, None] == grp_io[None, :]
    return tuple(a.astype(bf16) for a in (spread_io, spread_st, mask_m, mask_b, mask_c))


def _s5_call(u_perm, tabs, consts, l, *, rt):
    nch, bsz = u_perm.shape[0], u_perm.shape[1]
    kblk = CHUNK * LANE
    u2 = u_perm.reshape(nch * bsz, N_GBLK, CHUNK, LANE)
    io_spec = pl.BlockSpec((rt * bsz, None, CHUNK, LANE), lambda g, i: (i, g, 0, 0))
    mc, bc, cc, a_re, a_im = tabs
    spread_io, spread_st, mask_m, mask_b, mask_c = consts
    small_io, small_st = CHUNK * SSM_GROUP, 4 * SSM_STATE
    params = pltpu.CompilerParams(dimension_semantics=("arbitrary", "arbitrary"), vmem_limit_bytes=VMEM_LIMIT)
    whole = lambda a: pl.BlockSpec(a.shape, lambda g, i: (0,) * a.ndim)
    states = pl.pallas_call(
        _s5_state_kernel,
        grid=(N_GBLK, nch // rt),
        in_specs=[
            io_spec,
            pl.BlockSpec((None, None, kblk, small_st), lambda g, i: (l, g, 0, 0)),
            whole(spread_st), whole(mask_b),
        ],
        out_specs=pl.BlockSpec((rt * bsz, STATE_BLK), lambda g, i: (i, g)),
        out_shape=jax.ShapeDtypeStruct((nch * bsz, N_GBLK * STATE_BLK), f32),
        scratch_shapes=[pltpu.VMEM((kblk, STATE_BLK), bf16)],
        compiler_params=params,
        name="s5_state",
    )(u2, bc, spread_st, mask_b)
    n_scan = N_GBLK * STATE_BLK // SCAN_BLK
    carried = pl.pallas_call(
        functools.partial(_s5_scan_kernel, nch=nch, nb=bsz),
        grid=(n_scan,),
        in_specs=[
            pl.BlockSpec((nch * bsz, SCAN_BLK), lambda j: (0, j)),
            pl.BlockSpec((None, 1, SCAN_BLK // 2), lambda j: (l, 0, j)),
            pl.BlockSpec((None, 1, SCAN_BLK // 2), lambda j: (l, 0, j)),
        ],
        out_specs=pl.BlockSpec((nch * bsz, SCAN_BLK), lambda j: (0, j)),
        out_shape=jax.ShapeDtypeStruct((nch * bsz, N_GBLK * STATE_BLK), f32),
        compiler_params=pltpu.CompilerParams(dimension_semantics=("parallel",), vmem_limit_bytes=VMEM_LIMIT),
        name="s5_scan",
    )(states, a_re, a_im)
    y = pl.pallas_call(
        _s5_out_kernel,
        grid=(N_GBLK, nch // rt),
        in_specs=[
            io_spec,
            pl.BlockSpec((rt * bsz, STATE_BLK), lambda g, i: (i, g)),
            pl.BlockSpec((None, None, kblk, small_io), lambda g, i: (l, g, 0, 0)),
            pl.BlockSpec((None, None, STATE_BLK, small_io), lambda g, i: (l, g, 0, 0)),
            whole(spread_io), whole(mask_m), whole(mask_c),
        ],
        out_specs=io_spec,
        out_shape=jax.ShapeDtypeStruct((nch * bsz, N_GBLK, CHUNK, LANE), f32),
        scratch_shapes=[pltpu.VMEM((kblk, kblk), bf16), pltpu.VMEM((STATE_BLK, kblk), bf16)],
        compiler_params=params,
        name="s5_out",
    )(u2, carried, mc, cc, spread_io, mask_m, mask_c)
    return y.reshape(nch, bsz, N_GBLK, CHUNK, LANE)


def _s5_tables(lam_re, lam_im, log_step, b_re, b_im, c_re, c_im):
    L, G, N, P = CHUNK, SSM_GROUPS, SSM_STATE, SSM_GROUP
    hp = lax.Precision.HIGH
    step = jnp.exp(log_step)[..., None]
    zr, zi = lam_re * step, lam_im * step
    mag = jnp.exp(zr)
    abar_r, abar_i = mag * jnp.cos(zi), mag * jnp.sin(zi)
    nr, ni = abar_r - 1.0, abar_i
    den = lam_re * lam_re + lam_im * lam_im
    fr = (nr * lam_re + ni * lam_im) / den
    fi = (ni * lam_re - nr * lam_im) / den
    bbr = fr[..., None] * b_re - fi[..., None] * b_im
    bbi = fr[..., None] * b_im + fi[..., None] * b_re
    j = jnp.arange(L + 1, dtype=f32)[:, None, None, None]
    pmag = jnp.exp(j * zr)
    pr, pi = pmag * jnp.cos(j * zi), pmag * jnp.sin(j * zi)

    cpr = c_re[None, None] * pr[:L, :, :, None, :] - c_im[None, None] * pi[:L, :, :, None, :]
    cpi = c_re[None, None] * pi[:L, :, :, None, :] + c_im[None, None] * pr[:L, :, :, None, :]
    kern = jnp.einsum('jdgpn,dgnq->jdgpq', jnp.concatenate([cpr, -cpi], axis=-1),
                      jnp.concatenate([bbr, bbi], axis=-2), precision=hp)
    s_idx = jnp.arange(L)[:, None]
    t_idx = jnp.arange(L)[None, :]
    kf = kern[:, 0][jnp.clip(t_idx - s_idx, 0, L - 1)] * (s_idx <= t_idx)[:, :, None, None, None]
    kb = kern[:, 1][jnp.clip(s_idx - t_idx, 0, L - 1)] * (s_idx >= t_idx)[:, :, None, None, None]
    mg = jnp.transpose(kf + kb, (2, 0, 4, 1, 3))
    GB = GROUPS_PER_BLK
    mc = jnp.transpose(mg.reshape(N_GBLK, GB, L, P, L, P), (0, 2, 1, 3, 4, 5)).reshape(N_GBLK, L * LANE, L * P)

    psr = jnp.stack([pr[:L, 0][::-1], pr[:L, 1]])
    psi = jnp.stack([pi[:L, 0][::-1], pi[:L, 1]])
    wr = psr[..., None] * bbr[:, None] - psi[..., None] * bbi[:, None]
    wi = psr[..., None] * bbi[:, None] + psi[..., None] * bbr[:, None]
    wst = jnp.stack([wr, wi]).reshape(2, 2, L, N_GBLK, GB, N, P)
    bc = jnp.transpose(wst, (3, 2, 4, 6, 1, 0, 5)).reshape(N_GBLK, L * LANE, 4 * N)

    pcr = jnp.stack([pr[1:, 0], pr[1:, 1][::-1]])
    pci = jnp.stack([pi[1:, 0], pi[1:, 1][::-1]])
    ccr = c_re[None, None] * pcr[:, :, :, None, :] - c_im[None, None] * pci[:, :, :, None, :]
    cci = c_re[None, None] * pci[:, :, :, None, :] + c_im[None, None] * pcr[:, :, :, None, :]
    cst = jnp.stack([ccr, -cci]).reshape(2, 2, L, N_GBLK, 2, GB // 2, P, N)
    cc = jnp.transpose(cst, (3, 1, 4, 0, 5, 7, 2, 6)).reshape(N_GBLK, STATE_BLK, L * P)

    def scan_order(a):
        a = a.reshape(2, N_GBLK, 2, GB // 2, N)
        return jnp.transpose(a, (1, 0, 2, 3, 4)).reshape(1, -1)

    return mc.astype(bf16), bc.astype(bf16), cc.astype(bf16), scan_order(pr[L]), scan_order(pi[L])


def _mix_kernel(y_ref, u_ref, o_ref, gates_ref, x_ref, d_ref, wglu_ref, bglu_ref,
                wos_ref, wom_ref, wo_ref, out_ref, *, tm):
    parts = []
    for gb in range(N_GBLK):
        sl = slice(gb * LANE, (gb + 1) * LANE)
        yg = y_ref[:, 0, gb].reshape(tm, LANE)
        ug = u_ref[:, 0, gb].reshape(tm, LANE)
        parts.append(yg + d_ref[:, sl] * ug)
    y = jax.nn.gelu(jnp.concatenate(parts, axis=-1))
    y = y * jax.nn.sigmoid(_dot(y.astype(bf16), wglu_ref[...]) + bglu_ref[...])
    a = _dot(y.astype(bf16), wos_ref[...])
    m = _dot(o_ref[0], wom_ref[...])
    gates = gates_ref[0].astype(f32)
    merged = gates[:, :D_MODEL] * a + gates[:, D_MODEL:] * m
    out_ref[0] = x_ref[0] + _dot(merged.astype(bf16), wo_ref[...])


def _mix_call(y_perm, u_perm, o, gates, x, d, wglu, bglu, wos, wom, wo, l, *, tm):
    bsz, seq, _ = x.shape
    const = lambda shape: pl.BlockSpec(shape, lambda b, i: (0,) * len(shape))
    perm_spec = pl.BlockSpec((tm // CHUNK, 1, N_GBLK, CHUNK, LANE), lambda b, i: (i, b, 0, 0, 0))
    return pl.pallas_call(
        functools.partial(_mix_kernel, tm=tm),
        grid=(bsz, seq // tm),
        in_specs=[
            perm_spec, perm_spec,
            pl.BlockSpec((1, tm, MLA_WIDTH), lambda b, i: (b, i, 0)),
            pl.BlockSpec((1, tm, 2 * D_MODEL), lambda b, i: (b, i, 0)),
            pl.BlockSpec((1, tm, D_MODEL), lambda b, i: (b, i, 0)),
            const((1, SSM_WIDTH)),
            _layer_spec((SSM_WIDTH, SSM_WIDTH), l),
            const((1, SSM_WIDTH)),
            _layer_spec((SSM_WIDTH, D_MODEL), l),
            _layer_spec((MLA_WIDTH, D_MODEL), l),
            _layer_spec((D_MODEL, D_MODEL), l),
        ],
        out_specs=pl.BlockSpec((1, tm, D_MODEL), lambda b, i: (b, i, 0)),
        out_shape=jax.ShapeDtypeStruct(x.shape, f32),
        compiler_params=pltpu.CompilerParams(
            dimension_semantics=("parallel", "parallel"), vmem_limit_bytes=VMEM_LIMIT),
        name="mix",
    )(y_perm, u_perm, o, gates, x, d, wglu, bglu, wos, wom, wo)


def _ffn_kernel(x_ref, g_ref, w1_ref, w2_ref, out_ref, *, fb):
    x = x_ref[...]
    h = _rms(x, g_ref[...]).astype(bf16)
    acc = x
    for j in range(D_FF // fb):
        a = jnp.maximum(_dot(h, w1_ref[:, j * fb:(j + 1) * fb]), 0.0)
        acc = acc + _dot((a * a).astype(bf16), w2_ref[j * fb:(j + 1) * fb, :])
    out_ref[...] = acc


def _ffn_call(x2, g, w1, w2, l, *, tm, fb):
    ntok = x2.shape[0]
    return pl.pallas_call(
        functools.partial(_ffn_kernel, fb=fb),
        grid=(ntok // tm,),
        in_specs=[
            pl.BlockSpec((tm, D_MODEL), lambda i: (i, 0)),
            pl.BlockSpec((1, D_MODEL), lambda i: (0, 0)),
            pl.BlockSpec((None, D_MODEL, D_FF), lambda i: (l, 0, 0)),
            pl.BlockSpec((None, D_FF, D_MODEL), lambda i: (l, 0, 0)),
        ],
        out_specs=pl.BlockSpec((tm, D_MODEL), lambda i: (i, 0)),
        out_shape=jax.ShapeDtypeStruct(x2.shape, f32),
        compiler_params=pltpu.CompilerParams(dimension_semantics=("parallel",), vmem_limit_bytes=VMEM_LIMIT),
        name="ffn",
    )(x2, g, w1, w2)


def _rope_swap(a):
    half = QK_ROPE // 2
    return jnp.concatenate([a[..., half:], a[..., :half]], axis=-1)


def _place_rope(a):
    pad = [(0, 0)] * (a.ndim - 1) + [(QK_NOPE, HEAD_PAD - QK_HEAD)]
    return jnp.pad(a, pad)


def _pad_head(a, width):
    pad = [(0, 0)] * (a.ndim - 1) + [(0, HEAD_PAD - width)]
    a = jnp.pad(a, pad)
    return a.reshape(a.shape[:-2] + (MLA_HEADS * HEAD_PAD,))


def _layer_params(w, w_q, w_kv, qg, kg):
    o3 = SSM_WIDTH + Q_LORA + KV_LORA
    o4 = o3 + QK_ROPE
    wkr = w[:, o3:o4]
    win = jnp.concatenate([w[:, :o3], w[:, o4:], _place_rope(wkr), _place_rope(_rope_swap(wkr))], axis=1).astype(bf16)
    wq = w_q.reshape(Q_LORA, MLA_HEADS, QK_HEAD)
    wq_sw = _place_rope(_rope_swap(wq[..., QK_NOPE:])).reshape(Q_LORA, MLA_HEADS * HEAD_PAD)
    wq = jnp.concatenate([_pad_head(wq, QK_HEAD), wq_sw], axis=1).astype(bf16)
    wkv = w_kv.reshape(KV_LORA, MLA_HEADS, QK_NOPE + V_HEAD)
    wkv = jnp.concatenate([_pad_head(wkv[..., :QK_NOPE], QK_NOPE), _pad_head(wkv[..., QK_NOPE:], V_HEAD)],
                          axis=1).astype(bf16)
    one_col = jnp.zeros((HEAD_PAD,), f32).at[V_HEAD].set(1.0)
    zero = jnp.zeros((HEAD_PAD,), f32)
    hg = jnp.stack([
        jnp.pad(qg, (0, HEAD_PAD - QK_HEAD)), _place_rope(_rope_swap(qg[QK_NOPE:])),
        jnp.pad(kg, (0, HEAD_PAD - QK_HEAD)), _place_rope(_rope_swap(kg[QK_NOPE:])),
        one_col, zero, zero, zero])
    return win, wq, wkv, hg


def _rope_tables(seq):
    half = QK_ROPE // 2
    inv_freq = ROPE_THETA ** (-jnp.arange(half, dtype=f32) / half)
    ang = jnp.arange(seq, dtype=f32)[:, None] * inv_freq[None, :]
    cos, sin = jnp.cos(ang), jnp.sin(ang)
    ones = jnp.ones((seq, QK_NOPE), f32)
    zpad = jnp.zeros((seq, HEAD_PAD - QK_HEAD), f32)
    cos_t = jnp.concatenate([ones, cos, cos, zpad], axis=1)
    sin_t = jnp.concatenate([0.0 * ones, -sin, sin, zpad], axis=1)
    return cos_t, sin_t


def kernel(x, mix_norm_g, w_in, b_gate, ssm_lam_re, ssm_lam_im, ssm_log_step, ssm_b_re, ssm_b_im, ssm_c_re, ssm_c_im, ssm_d, w_glu, b_glu, w_out_ssm, q_norm_g, kv_norm_g, w_q_up, w_kv_up, q_head_g, k_head_g, w_out_mla, w_o, ffn_norm_g, w_ff1, w_ff2):
    bsz, seq, _ = x.shape
    depth = w_in.shape[0]
    nch = seq // CHUNK
    tm = min(512, seq)
    tm_mix = min(1024, seq)
    tq = min(512, seq)
    tk = min(4096, seq)
    rt = min(256, nch)
    assert 2 * bsz == 8, "the S5 chunk scan packs two chunks of `bsz` batch rows into one 8-sublane tile"
    cos_t, sin_t = _rope_tables(seq)
    win, wq, wkv, hg = jax.vmap(_layer_params)(w_in, w_q_up, w_kv_up, q_head_g, k_head_g)
    tabs = jax.vmap(_s5_tables)(ssm_lam_re, ssm_lam_im, ssm_log_step, ssm_b_re, ssm_b_im, ssm_c_re, ssm_c_im)
    consts = _s5_spread_constants()
    wglu, wos, wom, wo, w1, w2 = (a.astype(bf16) for a in (w_glu, w_out_ssm, w_out_mla, w_o, w_ff1, w_ff2))
    for l in range(depth):
        u_perm, gates, q, k, v = _pre_call(
            x, mix_norm_g[l][None], win, b_gate[l].reshape(1, 2 * D_MODEL), q_norm_g[l][None],
            kv_norm_g[l][None], wq, wkv, hg, cos_t, sin_t, l, tm=tm)
        o = _attn_call(q, k, v, tq=tq, tk=tk)
        y_perm = _s5_call(u_perm, tabs, consts, l, rt=rt)
        x = _mix_call(y_perm, u_perm, o, gates, x, ssm_d[l].reshape(1, SSM_WIDTH), wglu, b_glu[l][None],
                      wos, wom, wo, l, tm=tm_mix)
        x = _ffn_call(x.reshape(bsz * seq, D_MODEL), ffn_norm_g[l][None], w1, w2, l,
                      tm=tm, fb=1024).reshape(bsz, seq, D_MODEL)
    return x
```
